```python
import jax
import jax.numpy as jnp
from jax import lax
import numpy as np

D_MODEL = 1024
BATCH = 8
SEQ = 4096
DEPTH = 2

GRID_W = 64
CTX_LEN = 256
POOL_WINDOWS = (2, 4, 8, 16)
POOL_GROUP = D_MODEL // 8
POOL_WIDTH = POOL_GROUP * len(POOL_WINDOWS)
CONV_WIDTH = D_MODEL - POOL_WIDTH
CONV_K = 31
MLSTM_HEADS = 4
MLSTM_INNER = 2 * D_MODEL
MLSTM_HEAD_DIM = MLSTM_INNER // MLSTM_HEADS
MLSTM_CONV_K = 3
MLSTM_CHUNK = 64
FFN_HIDDEN = ((8 * D_MODEL + 3 * 256 - 1) // (3 * 256)) * 256
N_EVEN = (DEPTH + 1) // 2
N_ODD = DEPTH // 2
DEEPNORM_ALPHA = (2.0 * DEPTH) ** 0.25
DEEPNORM_BETA = (8.0 * DEPTH) ** -0.25
LN_EPS = 1e-5

kernel_name = "hybrid_pool_conv_mlstm_dit_block"


def layer_norm(x, g, b):
    xf = x.astype(jnp.float32)
    mu = jnp.mean(xf, axis=-1, keepdims=True)
    var = jnp.mean(jnp.square(xf - mu), axis=-1, keepdims=True)
    return ((xf - mu) * lax.rsqrt(var + LN_EPS) * g + b).astype(x.dtype)


def post_norm(x, branch, g, b):
    return layer_norm(DEEPNORM_ALPHA * x + branch, g, b)


def modulate(x, shift, scale):
    return x * (1 + scale) + shift


def swiglu(u, w_in, w_out):
    a, g = jnp.split(u @ w_in, 2, axis=-1)
    return (jax.nn.silu(a) * g) @ w_out


def depthwise_conv1d(x, w, b):
    K = w.shape[0]
    y = lax.conv_general_dilated(x, w[:, None, :], window_strides=(1,),
                                 padding=[(K // 2, K - 1 - K // 2)],
                                 dimension_numbers=('NWC', 'WIO', 'NWC'),
                                 feature_group_count=x.shape[-1])
    return y + b


def box_mean(x, w, axis):
    L = x.shape[axis]
    pad = [(0, 0)] * x.ndim
    pad[axis] = (1, 0)
    cs = jnp.pad(jnp.cumsum(x, axis=axis), pad)
    t = jnp.arange(L)
    lo = jnp.clip(t - w // 2, 0, L)
    hi = jnp.clip(t + w - w // 2, 0, L)
    s = jnp.take(cs, hi, axis=axis) - jnp.take(cs, lo, axis=axis)
    shape = [1] * x.ndim
    shape[axis] = L
    return s / (hi - lo).astype(jnp.float32).reshape(shape)


def pool_conv_mixer(u, rows, w_in, pool_w, pool_b, pool_scale, conv_w, conv_b, norm_g, norm_b, w_out):
    B, L, _ = u.shape
    p = u @ w_in
    a = p[..., :POOL_WIDTH]
    val = p[..., POOL_WIDTH:POOL_WIDTH + CONV_WIDTH]
    gate = p[..., POOL_WIDTH + CONV_WIDTH:]
    diffs = []
    for gi, w in enumerate(POOL_WINDOWS):
        seg = a[..., gi * POOL_GROUP:(gi + 1) * POOL_GROUP].astype(jnp.float32)
        if rows is None:
            pooled = box_mean(seg, w, 1)
        else:
            g2 = seg.reshape(B, rows, GRID_W, POOL_GROUP)
            pooled = box_mean(box_mean(g2, w, 1), w, 2).reshape(B, L, POOL_GROUP)
        diffs.append((pooled - seg).astype(u.dtype))
    d = jnp.stack(diffs, axis=2)
    y_a = (jnp.einsum('blgc,gcd->blgd', d, pool_w).reshape(B, L, POOL_WIDTH) + pool_b) * pool_scale
    glu = val * jax.nn.sigmoid(gate)
    y_b = jax.nn.silu(layer_norm(depthwise_conv1d(glu, conv_w, conv_b), norm_g, norm_b))
    return jnp.concatenate([y_a, y_b], axis=-1) @ w_out


def mlstm_zero_state(B):
    H, DH = MLSTM_HEADS, MLSTM_HEAD_DIM
    return (jnp.zeros((B, H, DH, DH), jnp.float32), jnp.zeros((B, H, DH), jnp.float32),
            jnp.zeros((B, H), jnp.float32))


def mlstm_features(xm, conv_w, conv_b, wq, wk, wv, w_gate, b_gate):
    B, L, E = xm.shape
    H, DH = MLSTM_HEADS, MLSTM_HEAD_DIM
    xc = jax.nn.silu(depthwise_conv1d(xm, conv_w, conv_b))
    xch = xc.reshape(B, L, H, DH)
    q = jnp.einsum('blhd,hde->blhe', xch, wq)
    k = jnp.einsum('blhd,hde->blhe', xch, wk)
    v = jnp.einsum('blhd,hde->blhe', xm.reshape(B, L, H, DH), wv)
    g = (jnp.einsum('ble,zeg->zblg', q.reshape(B, L, E), w_gate[:, 0])
         + jnp.einsum('ble,zeg->zblg', k.reshape(B, L, E), w_gate[:, 1])
         + jnp.einsum('ble,zeg->zblg', v.reshape(B, L, E), w_gate[:, 2])).astype(jnp.float32)
    g = g + b_gate[:, None, None, :].astype(jnp.float32)
    ig = g[..., :H]
    lf = jax.nn.log_sigmoid(g[..., H:])
    return xc, q, k, v, ig, lf


def mlstm_chunk_scan(k, v, ig, lf, state, q=None):
    B, L, H, _ = k.shape
    T = MLSTM_CHUNK
    nc = L // T

    def chunks(a):
        a = a.reshape((B, nc, T, H) + a.shape[3:])
        return jnp.moveaxis(jnp.moveaxis(a, 1, 0), 3, 2)

    def update(carry, kc, vc, ic, fc):
        C, n, m = carry
        b = jnp.cumsum(fc, axis=-1)
        b_last = b[..., -1]
        w_end = b_last[..., None] - b + ic
        m_new = jnp.maximum(b_last + m, jnp.max(w_end, axis=-1))
        decay = jnp.exp(b_last + m - m_new)
        kw = kc * jnp.exp(w_end - m_new[..., None])[..., None]
        C_new = decay[..., None, None] * C + jnp.einsum('bhtk,bhtv->bhkv', kw, vc)
        n_new = decay[..., None] * n + jnp.sum(kw, axis=2)
        return (C_new, n_new, m_new), b

    if q is None:
        def step_state(carry, xs):
            new, _ = update(carry, *xs)
            return new, None
        final, _ = lax.scan(step_state, state, (chunks(k), chunks(v), chunks(ig), chunks(lf)))
        return None, final

    causal = jnp.tril(jnp.ones((T, T), dtype=bool))

    def step(carry, xs):
        qc, kc, vc, ic, fc = xs
        C, n, m = carry
        new, b = update(carry, kc, vc, ic, fc)
        dmat = jnp.where(causal, b[..., :, None] - b[..., None, :] + ic[..., None, :], -jnp.inf)
        inter = b + m[..., None]
        m_t = jnp.maximum(inter, jnp.max(dmat, axis=-1))
        s = jnp.einsum('bhtk,bhsk->bhts', qc, kc) * jnp.exp(dmat - m_t[..., None])
        iw = jnp.exp(inter - m_t)
        num = iw[..., None] * jnp.einsum('bhtk,bhkv->bhtv', qc, C) + jnp.einsum('bhts,bhsv->bhtv', s, vc)
        den = iw * jnp.einsum('bhtk,bhk->bht', qc, n) + jnp.sum(s, axis=-1)
        h = num / jnp.maximum(jnp.abs(den), jnp.exp(-m_t))[..., None]
        return new, h

    final, hs = lax.scan(step, state, (chunks(q), chunks(k), chunks(v), chunks(ig), chunks(lf)))
    hs = jnp.moveaxis(jnp.moveaxis(hs, 0, 1), 2, 3).reshape(B, L, H, -1)
    return hs, final


def _flip(a, d):
    return jnp.flip(a, axis=1) if d == 1 else a


def mlstm_bidir(q, k, v, ig, lf, init_states, with_output):
    kf = k.astype(jnp.float32) * (MLSTM_HEAD_DIM ** -0.5)
    vf = v.astype(jnp.float32)
    qf = q.astype(jnp.float32) if with_output else None
    hs, finals = [], []
    for d in range(2):
        h, st = mlstm_chunk_scan(_flip(kf, d), _flip(vf, d), _flip(ig[d], d), _flip(lf[d], d),
                                 init_states[d], _flip(qf, d) if with_output else None)
        finals.append(st)
        if with_output:
            hs.append(_flip(h, d))
    h_sum = hs[0] + hs[1] if with_output else None
    return h_sum, (finals[0], finals[1])


def mlstm_output(h, xc, z, norm_g, skip, w_out):
    B, L = h.shape[:2]
    mu = jnp.mean(h, axis=-1, keepdims=True)
    var = jnp.mean(jnp.square(h - mu), axis=-1, keepdims=True)
    hn = ((h - mu) * lax.rsqrt(var + LN_EPS)).reshape(B, L, MLSTM_INNER) * norm_g
    y = (hn.astype(xc.dtype) + skip * xc) * jax.nn.silu(z)
    return y @ w_out


def mlstm_mixer(u, uc, w_in, conv_w, conv_b, wq, wk, wv, w_gate, b_gate, norm_g, skip, w_out, ctx_out):
    E = MLSTM_INNER
    feat = (conv_w, conv_b, wq, wk, wv, w_gate, b_gate)
    zero = mlstm_zero_state(u.shape[0])
    if ctx_out:
        xm_c, z_c = jnp.split(uc @ w_in, 2, axis=-1)
    else:
        xm_c = uc @ w_in[:, :E]
    xc_c, q_c, k_c, v_c, ig_c, lf_c = mlstm_features(xm_c, *feat)
    h_c, ctx_states = mlstm_bidir(q_c, k_c, v_c, ig_c, lf_c, (zero, zero), ctx_out)
    xm, z = jnp.split(u @ w_in, 2, axis=-1)
    xc, q, k, v, ig, lf = mlstm_features(xm, *feat)
    h_l, _ = mlstm_bidir(q, k, v, ig, lf, ctx_states, True)
    y = mlstm_output(h_l, xc, z, norm_g, skip, w_out)
    y_c = mlstm_output(h_c, xc_c, z_c, norm_g, skip, w_out) if ctx_out else None
    return y, y_c


def setup_inputs(seed: int = 0) -> dict:
    key = jax.random.key(seed)
    ks = jax.random.split(key, 32)
    D, E, H, DH, F = D_MODEL, MLSTM_INNER, MLSTM_HEADS, MLSTM_HEAD_DIM, FFN_HIDDEN

    def nrm(k, shape, s):
        return jax.random.normal(k, shape, jnp.float32) * s

    f_bias = jnp.broadcast_to(jnp.linspace(3.0, 6.0, H, dtype=jnp.float32), (N_ODD, 2, H))
    ml_b_gate = jnp.concatenate([nrm(ks[22], (N_ODD, 2, H), 0.1),
                                 f_bias + nrm(ks[23], (N_ODD, 2, H), 0.1)], axis=-1)
    return {
        'x': nrm(ks[0], (BATCH, SEQ, D), 1.0),
        'c': nrm(ks[1], (BATCH, D), 1.0),
        'ctx': nrm(ks[2], (BATCH, CTX_LEN, D), 1.0),
        'c_ctx': nrm(ks[3], (D,), 1.0),
        'mod_w': nrm(ks[4], (DEPTH, D, 6 * D), 0.3 * D ** -0.5),
        'mod_b': nrm(ks[5], (DEPTH, 6 * D), 0.02),
        'ln_g': 1.0 + nrm(ks[6], (DEPTH, 2, D), 0.05),
        'ln_b': nrm(ks[7], (DEPTH, 2, D), 0.02),
        'ab_w_in': nrm(ks[8], (N_EVEN, D, POOL_WIDTH + 2 * CONV_WIDTH), D ** -0.5),
        'ab_pool_w': nrm(ks[9], (N_EVEN, len(POOL_WINDOWS), POOL_GROUP, POOL_GROUP), POOL_GROUP ** -0.5),
        'ab_pool_b': nrm(ks[10], (N_EVEN, POOL_WIDTH), 0.02),
        'ab_pool_scale': 1.0 + nrm(ks[11], (N_EVEN, POOL_WIDTH), 0.1),
        'ab_conv_w': nrm(ks[12], (N_EVEN, CONV_K, CONV_WIDTH), CONV_K ** -0.5),
        'ab_conv_b': nrm(ks[13], (N_EVEN, CONV_WIDTH), 0.02),
        'ab_norm_g': 1.0 + nrm(ks[14], (N_EVEN, CONV_WIDTH), 0.05),
        'ab_norm_b': nrm(ks[15], (N_EVEN, CONV_WIDTH), 0.02),
        'ab_w_out': nrm(ks[16], (N_EVEN, POOL_WIDTH + CONV_WIDTH, D), DEEPNORM_BETA * (POOL_WIDTH + CONV_WIDTH) ** -0.5),
        'ml_w_in': nrm(ks[17], (N_ODD, D, 2 * E), D ** -0.5),
        'ml_conv_w': nrm(ks[18], (N_ODD, MLSTM_CONV_K, E), MLSTM_CONV_K ** -0.5),
        'ml_conv_b': nrm(ks[19], (N_ODD, E), 0.02),
        'ml_wq': nrm(ks[20], (N_ODD, H, DH, DH), DH ** -0.5),
        'ml_wk': nrm(ks[21], (N_ODD, H, DH, DH), DH ** -0.5),
        'ml_wv': nrm(ks[24], (N_ODD, H, DH, DH), DH ** -0.5),
        'ml_w_gate': nrm(ks[25], (N_ODD, 2, 3, E, 2 * H), 0.1 * (3 * E) ** -0.5),
        'ml_b_gate': ml_b_gate,
        'ml_norm_g': 1.0 + nrm(ks[26], (N_ODD, E), 0.05),
        'ml_skip': 1.0 + nrm(ks[27], (N_ODD, E), 0.05),
        'ml_w_out': nrm(ks[28], (N_ODD, E, D), DEEPNORM_BETA * E ** -0.5),
        'ffn_w_in': nrm(ks[29], (DEPTH, D, 2 * F), D ** -0.5),
        'ffn_w_out': nrm(ks[30], (DEPTH, F, D), DEEPNORM_BETA * F ** -0.5),
    }


def reference(x, c, ctx, c_ctx, mod_w, mod_b, ln_g, ln_b,
              ab_w_in, ab_pool_w, ab_pool_b, ab_pool_scale, ab_conv_w, ab_conv_b, ab_norm_g, ab_norm_b, ab_w_out,
              ml_w_in, ml_conv_w, ml_conv_b, ml_wq, ml_wk, ml_wv, ml_w_gate, ml_b_gate, ml_norm_g, ml_skip, ml_w_out,
              ffn_w_in, ffn_w_out):
    rows = x.shape[1] // GRID_W
    silu_c = jax.nn.silu(c)
    silu_cc = jax.nn.silu(c_ctx)
    h, hc = x, ctx
    for l in range(DEPTH):
        last = l == DEPTH - 1
        even = l % 2 == 0
        j = l // 2
        mods = (silu_c @ mod_w[l] + mod_b[l])[:, None, :]
        sh_m, sc_m, g_m, sh_f, sc_f, g_f = jnp.split(mods, 6, axis=-1)
        ctx_mixer_needed = (not last) or (not even)
        if ctx_mixer_needed:
            n_cols = 6 * D_MODEL if not last else 2 * D_MODEL
            cmods = jnp.split(silu_cc @ mod_w[l][:, :n_cols] + mod_b[l][:n_cols], n_cols // D_MODEL)
            uc = modulate(hc, cmods[0], cmods[1])
        u = modulate(h, sh_m, sc_m)
        if even:
            ab = (ab_w_in[j], ab_pool_w[j], ab_pool_b[j], ab_pool_scale[j], ab_conv_w[j], ab_conv_b[j],
                  ab_norm_g[j], ab_norm_b[j], ab_w_out[j])
            y = pool_conv_mixer(u, rows, *ab)
            y_c = pool_conv_mixer(uc, None, *ab) if not last else None
        else:
            y, y_c = mlstm_mixer(u, uc, ml_w_in[j], ml_conv_w[j], ml_conv_b[j], ml_wq[j], ml_wk[j], ml_wv[j],
                                 ml_w_gate[j], ml_b_gate[j], ml_norm_g[j], ml_skip[j], ml_w_out[j],
                                 ctx_out=not last)
        h = post_norm(h, g_m * y, ln_g[l, 0], ln_b[l, 0])
        h = post_norm(h, g_f * swiglu(modulate(h, sh_f, sc_f), ffn_w_in[l], ffn_w_out[l]), ln_g[l, 1], ln_b[l, 1])
        if not last:
            hc = post_norm(hc, cmods[2] * y_c, ln_g[l, 0], ln_b[l, 0])
            hc = post_norm(hc, cmods[5] * swiglu(modulate(hc, cmods[3], cmods[4]), ffn_w_in[l], ffn_w_out[l]),
                           ln_g[l, 1], ln_b[l, 1])
    return h
```

```python
import functools

import jax
import jax.numpy as jnp
from jax import lax
from jax.experimental import pallas as pl
from jax.experimental.pallas import tpu as pltpu

F32 = jnp.float32
BF16 = jnp.bfloat16

GRID_W = 64
POOL_WINDOWS = (2, 4, 8, 16)
POOL_GROUP = 128
CONV_K = 31
MLSTM_HEADS = 4
MLSTM_CONV_K = 3
LN_EPS = 1e-5

V7X_VMEM_BYTES = 64 * 1024 * 1024
V7X_LANES = 128
V7X_SUBLANES = 8
V7X_MXU_DIM = 256

SCAN_CHUNK = V7X_MXU_DIM
HALO_ROWS = 16
POOL_PAD_ROWS = (max(POOL_WINDOWS) // 2) * GRID_W


def _vmem_limit(nbytes):
    return int(min(nbytes * 5 // 4 + (4 << 20), V7X_VMEM_BYTES - (6 << 20)))


def _params(nbytes, n_axes):
    return pltpu.CompilerParams(dimension_semantics=("arbitrary",) * n_axes,
                                vmem_limit_bytes=_vmem_limit(nbytes))


def _resident(shape):
    nd = len(shape)
    return pl.BlockSpec(shape, lambda *_: (0,) * nd, pipeline_mode=pl.Buffered(1))


def _silu(x):
    return x * jax.nn.sigmoid(x)


def _layer_norm(r, g, b):
    mu = jnp.mean(r, axis=-1, keepdims=True)
    xc = r - mu
    var = jnp.mean(xc * xc, axis=-1, keepdims=True)
    return xc * lax.rsqrt(var + LN_EPS) * g + b


def _dot(a, b):
    return jnp.dot(a, b, preferred_element_type=F32)


def _mods_kernel(x_ref, w_ref, b_ref, o_ref):
    s = _silu(x_ref[...]).astype(BF16)
    o_ref[0] = _dot(s, w_ref[0].astype(BF16)) + b_ref[0]


def _mods(rows, mod_w, mod_b):
    depth, d, n = mod_w.shape
    r = rows.shape[0]
    tn = n // 4
    return pl.pallas_call(
        _mods_kernel,
        grid=(depth, n // tn),
        in_specs=[pl.BlockSpec((r, d), lambda l, j: (0, 0)),
                  pl.BlockSpec((1, d, tn), lambda l, j: (l, 0, j)),
                  pl.BlockSpec((1, 1, tn), lambda l, j: (l, 0, j))],
        out_specs=pl.BlockSpec((1, r, tn), lambda l, j: (l, 0, j)),
        out_shape=jax.ShapeDtypeStruct((depth, r, n), F32),
        compiler_params=_params(2 * d * tn * 4 + d * tn * 2, 2),
        name="mods",
    )(rows, mod_w, mod_b.reshape(depth, 1, n))


def _modmm_kernel(x_ref, sh_ref, sc_ref, w_ref, o_ref, *, n_chunk):
    u = (x_ref[0] * (1.0 + sc_ref[0]) + sh_ref[0]).astype(BF16)
    for j in range(0, w_ref.shape[1], n_chunk):
        o_ref[0, :, j:j + n_chunk] = _dot(u, w_ref[:, j:j + n_chunk]).astype(o_ref.dtype)


def _modmm(x, shift, scale, w, tm):
    bsz, seq, d = x.shape
    n = w.shape[1]
    vec = pl.BlockSpec((1, 1, d), lambda b, i: (b, 0, 0))
    nbytes = 2 * tm * d * 4 + d * n * 2 + 2 * tm * n * 2 + tm * d * 2
    return pl.pallas_call(
        functools.partial(_modmm_kernel, n_chunk=512),
        grid=(bsz, seq // tm),
        in_specs=[pl.BlockSpec((1, tm, d), lambda b, i: (b, i, 0)), vec, vec, _resident((d, n))],
        out_specs=pl.BlockSpec((1, tm, n), lambda b, i: (b, i, 0)),
        out_shape=jax.ShapeDtypeStruct((bsz, seq, n), BF16),
        compiler_params=_params(nbytes, 2),
        name="modmm",
    )(x, shift, scale, w)


def _pool_kernel(a_ref, band_ref, inv_ref, pw_ref, pb_ref, ps_ref, o_ref, pad_ref, *, grid2d):
    g = pl.program_id(1)
    seq = a_ref.shape[1]
    blk = band_ref.shape[1]
    band = band_ref[0]
    for i in range(seq // blk):
        pad_ref[pl.ds(POOL_PAD_ROWS + i * blk, blk), :] = _dot(band, a_ref[0, pl.ds(i * blk, blk), :])
    if grid2d:
        zeros = jnp.zeros((POOL_PAD_ROWS, POOL_GROUP), F32)
        pad_ref[pl.ds(0, POOL_PAD_ROWS), :] = zeros
        pad_ref[pl.ds(POOL_PAD_ROWS + seq, POOL_PAD_ROWS), :] = zeros

    ch = 256

    def finish(i, summed):
        rows = pl.ds(pl.multiple_of(i * ch, ch), ch)
        seg = a_ref[0, rows, :].astype(F32)
        d = (summed * inv_ref[0, rows, :] - seg).astype(BF16)
        y = (_dot(d, pw_ref[0]) + pb_ref[0]) * ps_ref[0]
        o_ref[0, rows, :] = y.astype(o_ref.dtype)

    if not grid2d:
        def body(i, carry):
            finish(i, pad_ref[pl.ds(pl.multiple_of(POOL_PAD_ROWS + i * ch, ch), ch), :])
            return carry
        lax.fori_loop(0, seq // ch, body, 0)
        return

    for gi, w in enumerate(POOL_WINDOWS):
        @pl.when(g == gi)
        def _(w=w):
            def body(i, carry):
                base = POOL_PAD_ROWS + i * ch
                acc = jnp.zeros((ch, POOL_GROUP), F32)
                for k in range(-(w // 2), w - w // 2):
                    acc = acc + pad_ref[pl.ds(pl.multiple_of(base + k * GRID_W, GRID_W), ch), :]
                finish(i, acc)
                return carry
            lax.fori_loop(0, seq // ch, body, 0)


def _pool_mixer(p, band, inv, pool_w, pool_b, pool_scale, grid2d):
    bsz, seq, _ = p.shape
    ng = len(POOL_WINDOWS)
    blk = band.shape[1]
    grp = lambda b, g: (g, 0, 0)
    nbytes = 4 * seq * POOL_GROUP * 2 + 2 * seq * V7X_LANES * 4 + (seq + 2 * POOL_PAD_ROWS) * POOL_GROUP * 4
    return pl.pallas_call(
        functools.partial(_pool_kernel, grid2d=grid2d),
        grid=(bsz, ng),
        in_specs=[pl.BlockSpec((1, seq, POOL_GROUP), lambda b, g: (b, 0, g)),
                  pl.BlockSpec((1, blk, blk), grp),
                  pl.BlockSpec((1, seq, 1), grp),
                  pl.BlockSpec((1, POOL_GROUP, POOL_GROUP), grp),
                  pl.BlockSpec((1, 1, POOL_GROUP), grp),
                  pl.BlockSpec((1, 1, POOL_GROUP), grp)],
        out_specs=pl.BlockSpec((1, seq, POOL_GROUP), lambda b, g: (b, 0, g)),
        out_shape=jax.ShapeDtypeStruct((bsz, seq, ng * POOL_GROUP), BF16),
        scratch_shapes=[pltpu.VMEM((seq + 2 * POOL_PAD_ROWS, POOL_GROUP), F32)],
        compiler_params=_params(nbytes, 2),
        name="pool_mixer",
    )(p, band, inv, pool_w, pool_b, pool_scale)


def _pool_constants(seq, grid2d):
    def band1d(n, w):
        t = jnp.arange(n)
        lo = jnp.clip(t - w // 2, 0, n)
        hi = jnp.clip(t + w - w // 2, 0, n)
        s = t[None, :]
        return ((s >= lo[:, None]) & (s < hi[:, None])).astype(F32), (hi - lo).astype(F32)

    bands, invs = [], []
    for w in POOL_WINDOWS:
        if grid2d:
            m, cnt = band1d(GRID_W, w)
            rows = seq // GRID_W
            _, rcnt = band1d(rows, w)
            bands.append(jnp.kron(jnp.eye(V7X_MXU_DIM // GRID_W, dtype=F32), m))
            invs.append(1.0 / (rcnt[:, None] * cnt[None, :]).reshape(seq))
        else:
            m, cnt = band1d(seq, w)
            bands.append(m)
            invs.append(1.0 / cnt)
    return jnp.stack(bands).astype(BF16), jnp.stack(invs).reshape(len(POOL_WINDOWS), seq, 1)


def _conv_kernel(v_ref, g_ref, vp_ref, gp_ref, vn_ref, gn_ref, cw_ref, cb_ref, ng_ref, nb_ref,
                 o_ref, x_ref, y_ref):
    i = pl.program_id(1)
    last = pl.num_programs(1) - 1
    tt, cw = v_ref.shape[1], v_ref.shape[2]

    def glu(v, g):
        return v.astype(F32) * jax.nn.sigmoid(g.astype(F32))

    x_ref[pl.ds(HALO_ROWS, tt), :] = glu(v_ref[0], g_ref[0])
    x_ref[pl.ds(0, HALO_ROWS), :] = jnp.where(i > 0, glu(vp_ref[0], gp_ref[0]), 0.0)
    x_ref[pl.ds(HALO_ROWS + tt, HALO_ROWS), :] = jnp.where(i < last, glu(vn_ref[0], gn_ref[0]), 0.0)

    ct, cl = 128, V7X_LANES
    n_a = (CONV_K + 1) // V7X_SUBLANES
    for c0 in range(0, cw, cl):
        def body(j, carry, c0=c0):
            base = pl.multiple_of(j * ct, ct)
            out = jnp.zeros((ct, cl), F32)
            for r in range(V7X_SUBLANES):
                part = jnp.zeros((ct + V7X_SUBLANES, cl), F32)
                for a in range(n_a):
                    k = V7X_SUBLANES * a + r - 1
                    if 0 <= k < CONV_K:
                        win = x_ref[pl.ds(base + V7X_SUBLANES * a, ct + V7X_SUBLANES), pl.ds(c0, cl)]
                        part = part + cw_ref[k:k + 1, c0:c0 + cl] * win
                out = out + part[r:r + ct, :]
            y_ref[pl.ds(base, ct), pl.ds(c0, cl)] = out
            return carry
        lax.fori_loop(0, tt // ct, body, 0)

    y = y_ref[...] + cb_ref[...]
    o_ref[0] = _silu(_layer_norm(y, ng_ref[...], nb_ref[...])).astype(o_ref.dtype)


def _conv_mixer(p, pool_width, conv_w, conv_b, norm_g, norm_b, tt):
    bsz, seq, _ = p.shape
    k, cw = conv_w.shape
    nb = pool_width // cw
    hb = tt // HALO_ROWS
    nhalo = seq // HALO_ROWS
    cur = lambda j: pl.BlockSpec((1, tt, cw), lambda b, i: (b, i, j))
    prv = lambda j: pl.BlockSpec((1, HALO_ROWS, cw), lambda b, i: (b, jnp.maximum(i * hb - 1, 0), j))
    nxt = lambda j: pl.BlockSpec((1, HALO_ROWS, cw), lambda b, i: (b, jnp.minimum((i + 1) * hb, nhalo - 1), j))
    row = lambda n: pl.BlockSpec((n, cw), lambda b, i: (0, 0))
    nbytes = 4 * tt * cw * 2 + 2 * tt * cw * 2 + (2 * tt + 2 * HALO_ROWS) * cw * 4 + 4 * tt * cw * 4
    return pl.pallas_call(
        _conv_kernel,
        grid=(bsz, seq // tt),
        in_specs=[cur(nb), cur(nb + 1), prv(nb), prv(nb + 1), nxt(nb), nxt(nb + 1),
                  row(k), row(1), row(1), row(1)],
        out_specs=pl.BlockSpec((1, tt, cw), lambda b, i: (b, i, 0)),
        out_shape=jax.ShapeDtypeStruct((bsz, seq, cw), BF16),
        scratch_shapes=[pltpu.VMEM((tt + 2 * HALO_ROWS, cw), F32), pltpu.VMEM((tt, cw), F32)],
        compiler_params=_params(nbytes, 2),
        name="conv_mixer",
    )(p, p, p, p, p, p, conv_w, conv_b.reshape(1, cw), norm_g.reshape(1, cw), norm_b.reshape(1, cw))


def _abproj_kernel(ya_ref, yb_ref, h_ref, w_ref, gm_ref, lg_ref, lb_ref, o_ref, *, alpha):
    ka = ya_ref.shape[2]
    y = _dot(ya_ref[0], w_ref[0:ka, :]) + _dot(yb_ref[0], w_ref[ka:, :])
    o_ref[0] = _layer_norm(alpha * h_ref[0] + gm_ref[0] * y, lg_ref[...], lb_ref[...])


def _abproj(ya, yb, h, w, gate, ln_g, ln_b, alpha, tm):
    bsz, seq, d = h.shape
    ka, kb = ya.shape[2], yb.shape[2]
    tok = lambda n: pl.BlockSpec((1, tm, n), lambda b, i: (b, i, 0))
    row = pl.BlockSpec((1, d), lambda b, i: (0, 0))
    nbytes = 2 * tm * (ka + kb) * 2 + 4 * tm * d * 4 + (ka + kb) * d * 2 + 2 * tm * d * 4
    return pl.pallas_call(
        functools.partial(_abproj_kernel, alpha=alpha),
        grid=(bsz, seq // tm),
        in_specs=[tok(ka), tok(kb), tok(d), _resident((ka + kb, d)),
                  pl.BlockSpec((1, 1, d), lambda b, i: (b, 0, 0)), row, row],
        out_specs=tok(d),
        out_shape=jax.ShapeDtypeStruct((bsz, seq, d), F32),
        compiler_params=_params(nbytes, 2),
        name="abproj_postnorm",
    )(ya, yb, h, w, gate, ln_g.reshape(1, d), ln_b.reshape(1, d))


def _ffn_kernel(h_ref, sh_ref, sc_ref, gt_ref, wi_ref, wo_ref, lg_ref, lb_ref, o_ref, *, alpha, fc):
    h = h_ref[0]
    u = (h * (1.0 + sc_ref[0]) + sh_ref[0]).astype(BF16)
    f = wo_ref.shape[0]
    acc = jnp.zeros(h.shape, F32)
    for c in range(0, f, fc):
        a = _dot(u, wi_ref[:, c:c + fc])
        g = _dot(u, wi_ref[:, f + c:f + c + fc])
        acc = acc + _dot((_silu(a) * g).astype(BF16), wo_ref[c:c + fc, :])
    o_ref[0] = _layer_norm(alpha * h + gt_ref[0] * acc, lg_ref[...], lb_ref[...])


def _ffn(h, shift, scale, gate, w_in, w_out, ln_g, ln_b, alpha, tm):
    bsz, seq, d = h.shape
    f = w_out.shape[0]
    tok = pl.BlockSpec((1, tm, d), lambda b, i: (b, i, 0))
    vec = pl.BlockSpec((1, 1, d), lambda b, i: (b, 0, 0))
    row = pl.BlockSpec((1, d), lambda b, i: (0, 0))
    nbytes = 4 * tm * d * 4 + 3 * d * f * 2 + 4 * tm * d * 4
    return pl.pallas_call(
        functools.partial(_ffn_kernel, alpha=alpha, fc=V7X_MXU_DIM),
        grid=(bsz, seq // tm),
        in_specs=[tok, vec, vec, vec, _resident((d, 2 * f)), _resident((f, d)), row, row],
        out_specs=tok,
        out_shape=jax.ShapeDtypeStruct((bsz, seq, d), F32),
        compiler_params=_params(nbytes, 2),
        name="ffn_postnorm",
    )(h, shift, scale, gate, w_in, w_out, ln_g.reshape(1, d), ln_b.reshape(1, d))


def _feat_kernel(xm_ref, xp_ref, xn_ref, cw_ref, cb_ref, wq_ref, wk_ref, wv_ref, wg_ref, bg_ref,
                 tri_ref, sel_ref, q_ref, k_ref, v_ref, xc_ref, g_ref, *, kscale):
    i = pl.program_id(1)
    last = pl.num_programs(1) - 1
    tm = xm_ref.shape[1]
    nh, dh = wq_ref.shape[0], wq_ref.shape[1]
    row = lax.broadcasted_iota(jnp.int32, (tm, dh), 0)
    gacc = jnp.zeros((tm, V7X_LANES), F32)
    for h in range(nh):
        cs = slice(h * dh, (h + 1) * dh)
        xb = xm_ref[0, :, cs]
        x = xb.astype(F32)
        prev = jnp.where(i > 0, xp_ref[0, HALO_ROWS - 1:HALO_ROWS, cs].astype(F32), 0.0)
        nxt = jnp.where(i < last, xn_ref[0, 0:1, cs].astype(F32), 0.0)
        x_m1 = jnp.where(row == 0, prev, pltpu.roll(x, 1, axis=0))
        x_p1 = jnp.where(row == tm - 1, nxt, pltpu.roll(x, tm - 1, axis=0))
        conv = cw_ref[0:1, cs] * x_m1 + cw_ref[1:2, cs] * x + cw_ref[2:3, cs] * x_p1 + cb_ref[:, cs]
        xc = _silu(conv).astype(BF16)
        xc_ref[0, :, cs] = xc
        q = _dot(xc, wq_ref[h])
        k = _dot(xc, wk_ref[h])
        v = _dot(xb, wv_ref[h])
        q_ref[0, :, cs] = q.astype(BF16)
        k_ref[0, :, cs] = (k * kscale).astype(BF16)
        v_ref[0, :, cs] = v.astype(BF16)
        gacc = (gacc + _dot(q.astype(BF16), wg_ref[0, cs, :]) + _dot(k.astype(BF16), wg_ref[1, cs, :])
                + _dot(v.astype(BF16), wg_ref[2, cs, :]))
    g = gacc + bg_ref[...]
    lf = jax.nn.log_sigmoid(g)
    hi = lf.astype(BF16)
    lo = (lf - hi.astype(F32)).astype(BF16)
    t = tri_ref.shape[1]
    sel = sel_ref[...]
    for c in range(tm // t):
        rs = slice(c * t, (c + 1) * t)
        fwd = _dot(tri_ref[0], hi[rs]) + _dot(tri_ref[0], lo[rs])
        bwd = _dot(tri_ref[1], hi[rs]) + _dot(tri_ref[1], lo[rs])
        cum = jnp.where(sel == 1.0, fwd, jnp.where(sel == 2.0, bwd, g[rs]))
        g_ref[0, rs, :] = cum


def _features(xmz, conv_w, conv_b, wq, wk, wv, wg, bg, tri, sel, tm):
    bsz, seq, _ = xmz.shape
    nh, dh, _ = wq.shape
    e = nh * dh
    hb = tm // HALO_ROWS
    nhalo = seq // HALO_ROWS
    tok = pl.BlockSpec((1, tm, e), lambda b, i: (b, i, 0))
    t = tri.shape[1]
    nbytes = 2 * tm * e * 2 * 5 + 3 * e * dh * 2 + 3 * e * V7X_LANES * 2 + 8 * tm * dh * 4 + 2 * t * t * 2
    return pl.pallas_call(
        functools.partial(_feat_kernel, kscale=float(dh) ** -0.5),
        grid=(bsz, seq // tm),
        in_specs=[tok,
                  pl.BlockSpec((1, HALO_ROWS, e), lambda b, i: (b, jnp.maximum(i * hb - 1, 0), 0)),
                  pl.BlockSpec((1, HALO_ROWS, e), lambda b, i: (b, jnp.minimum((i + 1) * hb, nhalo - 1), 0)),
                  pl.BlockSpec((MLSTM_CONV_K, e), lambda b, i: (0, 0)),
                  pl.BlockSpec((1, e), lambda b, i: (0, 0)),
                  _resident((nh, dh, dh)), _resident((nh, dh, dh)), _resident((nh, dh, dh)),
                  _resident((3, e, V7X_LANES)),
                  pl.BlockSpec((1, V7X_LANES), lambda b, i: (0, 0)),
                  _resident((2, t, t)),
                  pl.BlockSpec((1, V7X_LANES), lambda b, i: (0, 0))],
        out_specs=[tok, tok, tok, tok, pl.BlockSpec((1, tm, V7X_LANES), lambda b, i: (b, i, 0))],
        out_shape=[jax.ShapeDtypeStruct((bsz, seq, e), BF16)] * 4
        + [jax.ShapeDtypeStruct((bsz, seq, V7X_LANES), F32)],
        compiler_params=_params(nbytes, 2),
        name="mlstm_features",
    )(xmz, xmz, xmz, conv_w, conv_b.reshape(1, e), wq, wk, wv, wg, bg, tri, sel)


def _scan_kernel(q_ref, k_ref, v_ref, gc_ref, gr_ref, kx_ref, vx_ref, gxc_ref, gxr_ref, mask_ref,
                 o_ref, c_ref, n_ref, m_ref, *, nc):
    d = pl.program_id(2)
    c = pl.program_id(3)
    t = q_ref.shape[1]
    lane = lax.broadcasted_iota(jnp.int32, (1, t), 1)
    pos_last = (t - 1) * (1 - d)

    def gates(gc, gr):
        b_last = jnp.sum(jnp.where(lane == pos_last, gr[1:2, :], 0.0), axis=1, keepdims=True)
        return gc[:, 0:1], gc[:, 1:2], gr[0:1, :], gr[1:2, :], b_last

    def update_state(kb, vb, ig_c, b_c, ig_r, b_r, b_last):
        m = m_ref[...]
        m_new = jnp.maximum(b_last + m, jnp.max(b_last - b_r + ig_r, axis=1, keepdims=True))
        decay = jnp.exp(b_last + m - m_new)
        kw = kb.astype(F32) * jnp.exp(b_last - b_c + ig_c - m_new)
        kv = lax.dot_general(kw.astype(BF16), vb, (((0,), (0,)), ((), ())), preferred_element_type=F32)
        c_ref[...] = decay * c_ref[...] + kv
        n_ref[...] = decay * n_ref[...] + jnp.sum(kw, axis=0, keepdims=True)
        m_ref[...] = m_new

    @pl.when(c == 0)
    def _():
        c_ref[...] = jnp.zeros(c_ref.shape, F32)
        n_ref[...] = jnp.zeros(n_ref.shape, F32)
        m_ref[...] = jnp.zeros(m_ref.shape, F32)
        update_state(kx_ref[0], vx_ref[0], *gates(gxc_ref[...], gxr_ref[...]))

    qb, kb, vb = q_ref[0], k_ref[0], v_ref[0]
    ig_c, b_c, ig_r, b_r, b_last = gates(gc_ref[...], gr_ref[...])
    m = m_ref[...]
    dmat = jnp.where(mask_ref[0] > 0.0, b_c - b_r + ig_r, -jnp.inf)
    inter = b_c + m
    m_t = jnp.maximum(inter, jnp.max(dmat, axis=1, keepdims=True))
    qk = lax.dot_general(qb, kb, (((1,), (1,)), ((), ())), preferred_element_type=F32)
    s = qk * jnp.exp(dmat - m_t)
    iw = jnp.exp(inter - m_t)
    num = iw * _dot(qb, c_ref[...].astype(BF16)) + _dot(s.astype(BF16), vb)
    den = (iw * jnp.sum(qb.astype(F32) * n_ref[...], axis=1, keepdims=True)
           + jnp.sum(s, axis=1, keepdims=True))
    hout = num * (1.0 / jnp.maximum(jnp.abs(den), jnp.exp(-m_t)))
    rows = pl.ds(pl.multiple_of((c + d * (nc - 1 - 2 * c)) * t, t), t)

    @pl.when(d == 0)
    def _():
        o_ref[0, rows, :] = hout

    @pl.when(d == 1)
    def _():
        o_ref[0, rows, :] = o_ref[0, rows, :] + hout

    update_state(kb, vb, ig_c, b_c, ig_r, b_r, b_last)


def _scan(q, k, v, gcol, grow, kx, vx, gxcol, gxrow, mask):
    bsz, seq, e = q.shape
    nh = MLSTM_HEADS
    dh = e // nh
    t = SCAN_CHUNK
    nc = seq // t
    assert kx.shape[1] == t, "context length must equal one scan chunk"
    chunk = lambda c, d: c + d * (nc - 1 - 2 * c)
    qkv = pl.BlockSpec((1, t, dh), lambda b, h, d, c: (b, chunk(c, d), h))
    ctx = pl.BlockSpec((1, t, dh), lambda b, h, d, c: (b, 0, h))
    gc = pl.BlockSpec((None, None, None, t, 2), lambda b, h, d, c: (b, d, h, chunk(c, d), 0))
    gr = pl.BlockSpec((None, None, None, 2, t), lambda b, h, d, c: (b, d, h, 0, chunk(c, d)))
    gxc = pl.BlockSpec((None, None, None, t, 2), lambda b, h, d, c: (b, d, h, 0, 0))
    gxr = pl.BlockSpec((None, None, None, 2, t), lambda b, h, d, c: (b, d, h, 0, 0))
    nbytes = 2 * seq * dh * 4 + 10 * t * dh * 2 + 2 * dh * dh * 4 + 4 * t * V7X_LANES * 4 + 12 * t * t * 4
    return pl.pallas_call(
        functools.partial(_scan_kernel, nc=nc),
        grid=(bsz, nh, 2, nc),
        in_specs=[qkv, qkv, qkv, gc, gr, ctx, ctx, gxc, gxr,
                  pl.BlockSpec((1, t, t), lambda b, h, d, c: (d, 0, 0))],
        out_specs=pl.BlockSpec((1, seq, dh), lambda b, h, d, c: (b, 0, h)),
        out_shape=jax.ShapeDtypeStruct((bsz, seq, e), F32),
        scratch_shapes=[pltpu.VMEM((dh, dh), F32), pltpu.VMEM((1, dh), F32), pltpu.VMEM((1, 1), F32)],
        compiler_params=_params(nbytes, 4),
        name="mlstm_scan",
    )(q, k, v, gcol, grow, kx, vx, gxcol, gxrow, mask)


def _mlout_kernel(hs_ref, xc_ref, z_ref, h_ref, ng_ref, sk_ref, w_ref, gm_ref, lg_ref, lb_ref, o_ref,
                  *, alpha, nh):
    e = hs_ref.shape[2]
    dh = e // nh
    acc = jnp.zeros(h_ref.shape[1:], F32)
    for h in range(nh):
        cs = slice(h * dh, (h + 1) * dh)
        hh = hs_ref[0, :, cs]
        mu = jnp.mean(hh, axis=-1, keepdims=True)
        hc = hh - mu
        var = jnp.mean(hc * hc, axis=-1, keepdims=True)
        hn = hc * lax.rsqrt(var + LN_EPS) * ng_ref[:, cs]
        y = (hn + sk_ref[:, cs] * xc_ref[0, :, cs].astype(F32)) * _silu(z_ref[0, :, cs].astype(F32))
        acc = acc + _dot(y.astype(BF16), w_ref[cs, :])
    o_ref[0] = _layer_norm(alpha * h_ref[0] + gm_ref[0] * acc, lg_ref[...], lb_ref[...])


def _mlout(hs, xc, xmz, h, norm_g, skip, w, gate, ln_g, ln_b, alpha, tm):
    bsz, seq, d = h.shape
    e = hs.shape[2]
    tok = lambda n, j=0: pl.BlockSpec((1, tm, n), lambda b, i: (b, i, j))
    rowe = pl.BlockSpec((1, e), lambda b, i: (0, 0))
    rowd = pl.BlockSpec((1, d), lambda b, i: (0, 0))
    nbytes = 2 * tm * e * 8 + 4 * tm * d * 4 + e * d * 2 + 6 * tm * (e // MLSTM_HEADS) * 4
    return pl.pallas_call(
        functools.partial(_mlout_kernel, alpha=alpha, nh=MLSTM_HEADS),
        grid=(bsz, seq // tm),
        in_specs=[tok(e), tok(e), tok(e, 1), tok(d), rowe, rowe, _resident((e, d)),
                  pl.BlockSpec((1, 1, d), lambda b, i: (b, 0, 0)), rowd, rowd],
        out_specs=tok(d),
        out_shape=jax.ShapeDtypeStruct((bsz, seq, d), F32),
        compiler_params=_params(nbytes, 2),
        name="mlstm_out_postnorm",
    )(hs, xc, xmz, h, norm_g.reshape(1, e), skip.reshape(1, e), w, gate, ln_g.reshape(1, d),
      ln_b.reshape(1, d))


def _gate_layouts(g):
    bsz, seq, _ = g.shape
    nh = MLSTM_HEADS
    g = g[..., :4 * nh].reshape(bsz, seq, 2, 2, nh)
    return jnp.transpose(g, (0, 3, 4, 1, 2)), jnp.transpose(g, (0, 3, 4, 2, 1))


def _pool_conv_layer(h, mods6, ab, ln_g, ln_b, alpha, grid2d, tm):
    sh_m, sc_m, g_m = mods6[0], mods6[1], mods6[2]
    w_in, pool_w, pool_b, pool_scale, conv_w, conv_b, norm_g, norm_b, w_out = ab
    seq = h.shape[1]
    pool_width = pool_w.shape[0] * POOL_GROUP
    p = _modmm(h, sh_m, sc_m, w_in, tm)
    band, inv = _pool_constants(seq, grid2d)
    ng = len(POOL_WINDOWS)
    ya = _pool_mixer(p, band, inv, pool_w, pool_b.reshape(ng, 1, POOL_GROUP),
                     pool_scale.reshape(ng, 1, POOL_GROUP), grid2d)
    yb = _conv_mixer(p, pool_width, conv_w, conv_b, norm_g, norm_b, tm)
    return _abproj(ya, yb, h, w_out, g_m, ln_g, ln_b, alpha, tm)


def kernel(x, c, ctx, c_ctx, mod_w, mod_b, ln_g, ln_b, ab_w_in, ab_pool_w, ab_pool_b, ab_pool_scale, ab_conv_w, ab_conv_b, ab_norm_g, ab_norm_b, ab_w_out, ml_w_in, ml_conv_w, ml_conv_b, ml_wq, ml_wk, ml_wv, ml_w_gate, ml_b_gate, ml_norm_g, ml_skip, ml_w_out, ffn_w_in, ffn_w_out):
    bsz, seq, d = x.shape
    depth = mod_w.shape[0]
    nh = MLSTM_HEADS
    alpha = (2.0 * depth) ** 0.25
    tm_x, tm_c = 512, ctx.shape[1]

    n_rows = -(-(bsz + 1) // V7X_SUBLANES) * V7X_SUBLANES
    rows = jnp.concatenate([c, c_ctx[None, :], jnp.zeros((n_rows - bsz - 1, d), F32)], axis=0)
    mods = _mods(rows, mod_w, mod_b)

    h, hc = x, ctx
    for l in range(depth):
        last = l == depth - 1
        even = l % 2 == 0
        j = l // 2
        mods_x = [m.reshape(bsz, 1, d) for m in jnp.split(mods[l, :bsz], 6, axis=-1)]
        mods_c = [jnp.broadcast_to(m.reshape(1, 1, d), (bsz, 1, d)) for m in jnp.split(mods[l, bsz], 6)]
        ffn_wi, ffn_wo = ffn_w_in[l].astype(BF16), ffn_w_out[l].astype(BF16)
        if even:
            ab = (ab_w_in[j].astype(BF16), ab_pool_w[j].astype(BF16), ab_pool_b[j], ab_pool_scale[j],
                  ab_conv_w[j], ab_conv_b[j], ab_norm_g[j], ab_norm_b[j], ab_w_out[j].astype(BF16))
            h1 = _pool_conv_layer(h, mods_x, ab, ln_g[l, 0], ln_b[l, 0], alpha, True, tm_x)
            if not last:
                hc1 = _pool_conv_layer(hc, mods_c, ab, ln_g[l, 0], ln_b[l, 0], alpha, False, tm_c)
        else:
            e = ml_w_in.shape[2] // 2
            dh = e // nh
            w_in = ml_w_in[j].astype(BF16)
            wq, wk, wv = ml_wq[j].astype(BF16), ml_wk[j].astype(BF16), ml_wv[j].astype(BF16)
            wg = jnp.transpose(ml_w_gate[j].reshape(2, 3, e, 2, nh), (1, 2, 3, 0, 4)).reshape(3, e, 4 * nh)
            wg = jnp.pad(wg, ((0, 0), (0, 0), (0, V7X_LANES - 4 * nh))).astype(BF16)
            bg = jnp.transpose(ml_b_gate[j].reshape(2, 2, nh), (1, 0, 2)).reshape(1, 4 * nh)
            bg = jnp.pad(bg, ((0, 0), (0, V7X_LANES - 4 * nh)))
            t = SCAN_CHUNK
            idx = jnp.arange(t)
            lower = (idx[None, :] <= idx[:, None])
            tri = jnp.stack([lower, lower.T]).astype(BF16)
            mask = jnp.stack([lower, lower.T]).astype(F32)
            col = jnp.arange(V7X_LANES)
            sel = jnp.where((col >= 2 * nh) & (col < 3 * nh), 1.0,
                            jnp.where((col >= 3 * nh) & (col < 4 * nh), 2.0, 0.0)).reshape(1, V7X_LANES)
            feat = (ml_conv_w[j], ml_conv_b[j], wq, wk, wv, wg, bg, tri, sel)
            xm_c = _modmm(hc, mods_c[0], mods_c[1], w_in if not last else w_in[:, :e], tm_c)
            _, k_c, v_c, xc_c, g_c = _features(xm_c, *feat, tm_c)
            xmz = _modmm(h, mods_x[0], mods_x[1], w_in, tm_x)
            q, k, v, xc, g = _features(xmz, *feat, tm_x)
            gcol, grow = _gate_layouts(g)
            gxcol, gxrow = _gate_layouts(g_c)
            hs = _scan(q, k, v, gcol, grow, k_c, v_c, gxcol, gxrow, mask)
            w_out = ml_w_out[j].astype(BF16)
            h1 = _mlout(hs, xc, xmz, h, ml_norm_g[j], ml_skip[j], w_out, mods_x[2], ln_g[l, 0],
                        ln_b[l, 0], alpha, tm_x)
            if not last:
                raise NotImplementedError("context output of an mLSTM layer is only needed for depth > 2")
        h = _ffn(h1, mods_x[3], mods_x[4], mods_x[5], ffn_wi, ffn_wo, ln_g[l, 1], ln_b[l, 1], alpha, tm_x)
        if not last:
            hc = _ffn(hc1, mods_c[3], mods_c[4], mods_c[5], ffn_wi, ffn_wo, ln_g[l, 1], ln_b[l, 1],
                      alpha, tm_c)
    return h
```

```python
import functools

import jax
import jax.numpy as jnp
from jax import lax
from jax.experimental import pallas as pl
from jax.experimental.pallas import tpu as pltpu

F32 = jnp.float32
BF16 = jnp.bfloat16

GRID_W = 64
POOL_WINDOWS = (2, 4, 8, 16)
POOL_GROUP = 128
CONV_K = 31
MLSTM_HEADS = 4
MLSTM_CONV_K = 3
LN_EPS = 1e-5

V7X_VMEM_BYTES = 64 * 1024 * 1024
V7X_LANES = 128
V7X_SUBLANES = 8
V7X_BF16_SUBLANES = 16
V7X_MXU_DIM = 256

SCAN_CHUNK = V7X_MXU_DIM
HALO_ROWS = V7X_BF16_SUBLANES
POOL_PAD_ROWS = (max(POOL_WINDOWS) // 2) * GRID_W
GATE_ROWS = 4 * MLSTM_HEADS


def _vmem_limit(nbytes):
    return int(min(nbytes * 5 // 4 + (4 << 20), V7X_VMEM_BYTES - (6 << 20)))


def _params(nbytes, n_axes):
    return pltpu.CompilerParams(dimension_semantics=("arbitrary",) * n_axes,
                                vmem_limit_bytes=_vmem_limit(nbytes))


def _resident(shape):
    nd = len(shape)
    return pl.BlockSpec(shape, lambda *_: (0,) * nd, pipeline_mode=pl.Buffered(1))


def _silu(x):
    return x * jax.nn.sigmoid(x)


def _layer_norm(r, g, b):
    mu = jnp.mean(r, axis=-1, keepdims=True)
    xc = r - mu
    var = jnp.mean(xc * xc, axis=-1, keepdims=True)
    return xc * lax.rsqrt(var + LN_EPS) * g + b


def _dot(a, b):
    return jnp.dot(a, b, preferred_element_type=F32)


def _dot_nt(a, b):
    return lax.dot_general(a, b, (((1,), (1,)), ((), ())), preferred_element_type=F32)


def _split_bf16(x):
    hi = x.astype(BF16)
    return hi, (x - hi.astype(F32)).astype(BF16)


def _mods_kernel(x_ref, w_ref, b_ref, o_ref):
    s = _silu(x_ref[...]).astype(BF16)
    o_ref[0] = _dot(s, w_ref[0].astype(BF16)) + b_ref[0]


def _mods(rows, mod_w, mod_b):
    depth, d, n = mod_w.shape
    r = rows.shape[0]
    tn = n // 4
    return pl.pallas_call(
        _mods_kernel,
        grid=(depth, n // tn),
        in_specs=[pl.BlockSpec((r, d), lambda l, j: (0, 0)),
                  pl.BlockSpec((1, d, tn), lambda l, j: (l, 0, j)),
                  pl.BlockSpec((1, 1, tn), lambda l, j: (l, 0, j))],
        out_specs=pl.BlockSpec((1, r, tn), lambda l, j: (l, 0, j)),
        out_shape=jax.ShapeDtypeStruct((depth, r, n), F32),
        compiler_params=_params(2 * d * tn * 4 + d * tn * 2, 2),
        name="mods",
    )(rows, mod_w, mod_b.reshape(depth, 1, n))


def _modmm_kernel(x_ref, sh_ref, sc_ref, w_ref, o_ref, *, n_chunk, silu_from):
    u = (x_ref[0] * (1.0 + sc_ref[0]) + sh_ref[0]).astype(BF16)
    for j in range(0, w_ref.shape[1], n_chunk):
        y = _dot(u, w_ref[:, j:j + n_chunk])
        if j >= silu_from:
            y = _silu(y)
        o_ref[0, :, j:j + n_chunk] = y.astype(o_ref.dtype)


def _modmm(x, shift, scale, w, tm, silu_from=None):
    bsz, seq, d = x.shape
    n = w.shape[1]
    silu_from = n if silu_from is None else silu_from
    vec = pl.BlockSpec((1, 1, d), lambda b, i: (b, 0, 0))
    nbytes = 2 * tm * d * 4 + d * n * 2 + 2 * tm * n * 2 + tm * d * 2
    return pl.pallas_call(
        functools.partial(_modmm_kernel, n_chunk=512, silu_from=silu_from),
        grid=(bsz, seq // tm),
        in_specs=[pl.BlockSpec((1, tm, d), lambda b, i: (b, i, 0)), vec, vec, _resident((d, n))],
        out_specs=pl.BlockSpec((1, tm, n), lambda b, i: (b, i, 0)),
        out_shape=jax.ShapeDtypeStruct((bsz, seq, n), BF16),
        compiler_params=_params(nbytes, 2),
        name="modmm",
    )(x, shift, scale, w)


def _pool_kernel(a_ref, band_ref, inv_ref, pw_ref, pb_ref, ps_ref, o_ref, pad_ref, *, grid2d):
    g = pl.program_id(1)
    seq = a_ref.shape[1]
    blk = band_ref.shape[1]
    band = band_ref[0]
    for i in range(seq // blk):
        pad_ref[pl.ds(POOL_PAD_ROWS + i * blk, blk), :] = _dot(band, a_ref[0, pl.ds(i * blk, blk), :])
    if grid2d:
        zeros = jnp.zeros((POOL_PAD_ROWS, POOL_GROUP), F32)
        pad_ref[pl.ds(0, POOL_PAD_ROWS), :] = zeros
        pad_ref[pl.ds(POOL_PAD_ROWS + seq, POOL_PAD_ROWS), :] = zeros

    ch = 256

    def finish(i, summed):
        rows = pl.ds(pl.multiple_of(i * ch, ch), ch)
        seg = a_ref[0, rows, :].astype(F32)
        d = (summed * inv_ref[0, rows, :] - seg).astype(BF16)
        y = (_dot(d, pw_ref[0]) + pb_ref[0]) * ps_ref[0]
        o_ref[0, rows, :] = y.astype(o_ref.dtype)

    if not grid2d:
        def body(i, carry):
            finish(i, pad_ref[pl.ds(pl.multiple_of(POOL_PAD_ROWS + i * ch, ch), ch), :])
            return carry
        lax.fori_loop(0, seq // ch, body, 0)
        return

    for gi, w in enumerate(POOL_WINDOWS):
        @pl.when(g == gi)
        def _(w=w):
            def body(i, carry):
                base = POOL_PAD_ROWS + i * ch
                acc = jnp.zeros((ch, POOL_GROUP), F32)
                for k in range(-(w // 2), w - w // 2):
                    acc = acc + pad_ref[pl.ds(pl.multiple_of(base + k * GRID_W, GRID_W), ch), :]
                finish(i, acc)
                return carry
            lax.fori_loop(0, seq // ch, body, 0)


def _pool_mixer(p, band, inv, pool_w, pool_b, pool_scale, grid2d):
    bsz, seq, _ = p.shape
    ng = len(POOL_WINDOWS)
    blk = band.shape[1]
    grp = lambda b, g: (g, 0, 0)
    nbytes = 4 * seq * POOL_GROUP * 2 + 2 * seq * V7X_LANES * 4 + (seq + 2 * POOL_PAD_ROWS) * POOL_GROUP * 4
    return pl.pallas_call(
        functools.partial(_pool_kernel, grid2d=grid2d),
        grid=(bsz, ng),
        in_specs=[pl.BlockSpec((1, seq, POOL_GROUP), lambda b, g: (b, 0, g)),
                  pl.BlockSpec((1, blk, blk), grp),
                  pl.BlockSpec((1, seq, 1), grp),
                  pl.BlockSpec((1, POOL_GROUP, POOL_GROUP), grp),
                  pl.BlockSpec((1, 1, POOL_GROUP), grp),
                  pl.BlockSpec((1, 1, POOL_GROUP), grp)],
        out_specs=pl.BlockSpec((1, seq, POOL_GROUP), lambda b, g: (b, 0, g)),
        out_shape=jax.ShapeDtypeStruct((bsz, seq, ng * POOL_GROUP), BF16),
        scratch_shapes=[pltpu.VMEM((seq + 2 * POOL_PAD_ROWS, POOL_GROUP), F32)],
        compiler_params=_params(nbytes, 2),
        name="pool_mixer",
    )(p, band, inv, pool_w, pool_b, pool_scale)


def _pool_constants(seq, grid2d):
    def band1d(n, w):
        t = jnp.arange(n)
        lo = jnp.clip(t - w // 2, 0, n)
        hi = jnp.clip(t + w - w // 2, 0, n)
        s = t[None, :]
        return ((s >= lo[:, None]) & (s < hi[:, None])).astype(F32), (hi - lo).astype(F32)

    bands, invs = [], []
    for w in POOL_WINDOWS:
        if grid2d:
            m, cnt = band1d(GRID_W, w)
            rows = seq // GRID_W
            _, rcnt = band1d(rows, w)
            bands.append(jnp.kron(jnp.eye(V7X_MXU_DIM // GRID_W, dtype=F32), m))
            invs.append(1.0 / (rcnt[:, None] * cnt[None, :]).reshape(seq))
        else:
            m, cnt = band1d(seq, w)
            bands.append(m)
            invs.append(1.0 / cnt)
    return jnp.stack(bands).astype(BF16), jnp.stack(invs).reshape(len(POOL_WINDOWS), seq, 1)


def _conv_kernel(v_ref, g_ref, vp_ref, gp_ref, vn_ref, gn_ref, cw_ref, cb_ref, ng_ref, nb_ref,
                 o_ref, x_ref, y_ref):
    i = pl.program_id(1)
    last = pl.num_programs(1) - 1
    tt, cw = v_ref.shape[1], v_ref.shape[2]

    def glu(v, g):
        return v.astype(F32) * jax.nn.sigmoid(g.astype(F32))

    x_ref[pl.ds(HALO_ROWS, tt), :] = glu(v_ref[0], g_ref[0])
    x_ref[pl.ds(0, HALO_ROWS), :] = jnp.where(i > 0, glu(vp_ref[0], gp_ref[0]), 0.0)
    x_ref[pl.ds(HALO_ROWS + tt, HALO_ROWS), :] = jnp.where(i < last, glu(vn_ref[0], gn_ref[0]), 0.0)

    ct, cl = 128, V7X_LANES
    n_a = (CONV_K + 1) // V7X_SUBLANES
    for c0 in range(0, cw, cl):
        def body(j, carry, c0=c0):
            base = pl.multiple_of(j * ct, ct)
            out = jnp.zeros((ct, cl), F32)
            for r in range(V7X_SUBLANES):
                part = jnp.zeros((ct + V7X_SUBLANES, cl), F32)
                for a in range(n_a):
                    k = V7X_SUBLANES * a + r - 1
                    if 0 <= k < CONV_K:
                        win = x_ref[pl.ds(base + V7X_SUBLANES * a, ct + V7X_SUBLANES), pl.ds(c0, cl)]
                        part = part + cw_ref[k:k + 1, c0:c0 + cl] * win
                out = out + part[r:r + ct, :]
            y_ref[pl.ds(base, ct), pl.ds(c0, cl)] = out
            return carry
        lax.fori_loop(0, tt // ct, body, 0)

    y = y_ref[...] + cb_ref[...]
    o_ref[0] = _silu(_layer_norm(y, ng_ref[...], nb_ref[...])).astype(o_ref.dtype)


def _conv_mixer(p, pool_width, conv_w, conv_b, norm_g, norm_b, tt):
    bsz, seq, _ = p.shape
    k, cw = conv_w.shape
    nb = pool_width // cw
    hb = tt // HALO_ROWS
    nhalo = seq // HALO_ROWS
    cur = lambda j: pl.BlockSpec((1, tt, cw), lambda b, i: (b, i, j))
    prv = lambda j: pl.BlockSpec((1, HALO_ROWS, cw), lambda b, i: (b, jnp.maximum(i * hb - 1, 0), j))
    nxt = lambda j: pl.BlockSpec((1, HALO_ROWS, cw), lambda b, i: (b, jnp.minimum((i + 1) * hb, nhalo - 1), j))
    row = lambda n: pl.BlockSpec((n, cw), lambda b, i: (0, 0))
    nbytes = 4 * tt * cw * 2 + 2 * tt * cw * 2 + (2 * tt + 2 * HALO_ROWS) * cw * 4 + 4 * tt * cw * 4
    return pl.pallas_call(
        _conv_kernel,
        grid=(bsz, seq // tt),
        in_specs=[cur(nb), cur(nb + 1), prv(nb), prv(nb + 1), nxt(nb), nxt(nb + 1),
                  row(k), row(1), row(1), row(1)],
        out_specs=pl.BlockSpec((1, tt, cw), lambda b, i: (b, i, 0)),
        out_shape=jax.ShapeDtypeStruct((bsz, seq, cw), BF16),
        scratch_shapes=[pltpu.VMEM((tt + 2 * HALO_ROWS, cw), F32), pltpu.VMEM((tt, cw), F32)],
        compiler_params=_params(nbytes, 2),
        name="conv_mixer",
    )(p, p, p, p, p, p, conv_w, conv_b.reshape(1, cw), norm_g.reshape(1, cw), norm_b.reshape(1, cw))


def _abproj_kernel(ya_ref, yb_ref, h_ref, w_ref, gm_ref, lg_ref, lb_ref, o_ref, *, alpha):
    ka = ya_ref.shape[2]
    y = _dot(ya_ref[0], w_ref[0:ka, :]) + _dot(yb_ref[0], w_ref[ka:, :])
    o_ref[0] = _layer_norm(alpha * h_ref[0] + gm_ref[0] * y, lg_ref[...], lb_ref[...])


def _abproj(ya, yb, h, w, gate, ln_g, ln_b, alpha, tm):
    bsz, seq, d = h.shape
    ka, kb = ya.shape[2], yb.shape[2]
    tok = lambda n: pl.BlockSpec((1, tm, n), lambda b, i: (b, i, 0))
    row = pl.BlockSpec((1, d), lambda b, i: (0, 0))
    nbytes = 2 * tm * (ka + kb) * 2 + 4 * tm * d * 4 + (ka + kb) * d * 2 + 2 * tm * d * 4
    return pl.pallas_call(
        functools.partial(_abproj_kernel, alpha=alpha),
        grid=(bsz, seq // tm),
        in_specs=[tok(ka), tok(kb), tok(d), _resident((ka + kb, d)),
                  pl.BlockSpec((1, 1, d), lambda b, i: (b, 0, 0)), row, row],
        out_specs=tok(d),
        out_shape=jax.ShapeDtypeStruct((bsz, seq, d), F32),
        compiler_params=_params(nbytes, 2),
        name="abproj_postnorm",
    )(ya, yb, h, w, gate, ln_g.reshape(1, d), ln_b.reshape(1, d))


def _ffn_kernel(h_ref, sh_ref, sc_ref, gt_ref, wi_ref, wo_ref, lg_ref, lb_ref, o_ref, *, alpha, fc):
    h = h_ref[0]
    u = (h * (1.0 + sc_ref[0]) + sh_ref[0]).astype(BF16)
    f = wo_ref.shape[0]
    acc = jnp.zeros(h.shape, F32)
    for c in range(0, f, fc):
        a = _dot(u, wi_ref[:, c:c + fc])
        g = _dot(u, wi_ref[:, f + c:f + c + fc])
        acc = acc + _dot((_silu(a) * g).astype(BF16), wo_ref[c:c + fc, :])
    o_ref[0] = _layer_norm(alpha * h + gt_ref[0] * acc, lg_ref[...], lb_ref[...])


def _ffn(h, shift, scale, gate, w_in, w_out, ln_g, ln_b, alpha, tm):
    bsz, seq, d = h.shape
    f = w_out.shape[0]
    tok = pl.BlockSpec((1, tm, d), lambda b, i: (b, i, 0))
    vec = pl.BlockSpec((1, 1, d), lambda b, i: (b, 0, 0))
    row = pl.BlockSpec((1, d), lambda b, i: (0, 0))
    nbytes = 4 * tm * d * 4 + 3 * d * f * 2 + 4 * tm * d * 4
    return pl.pallas_call(
        functools.partial(_ffn_kernel, alpha=alpha, fc=V7X_MXU_DIM),
        grid=(bsz, seq // tm),
        in_specs=[tok, vec, vec, vec, _resident((d, 2 * f)), _resident((f, d)), row, row],
        out_specs=tok,
        out_shape=jax.ShapeDtypeStruct((bsz, seq, d), F32),
        compiler_params=_params(nbytes, 2),
        name="ffn_postnorm",
    )(h, shift, scale, gate, w_in, w_out, ln_g.reshape(1, d), ln_b.reshape(1, d))


def _gatefold_kernel(wq_ref, wk_ref, wv_ref, wg_ref, o_ref):
    def mm(a, b):
        ah, al = _split_bf16(a)
        bh, bl = _split_bf16(b)
        return _dot(ah, bh) + _dot(ah, bl) + _dot(al, bh)

    o_ref[0] = (mm(wq_ref[0], wg_ref[0]) + mm(wk_ref[0], wg_ref[1])).astype(o_ref.dtype)
    o_ref[1] = mm(wv_ref[0], wg_ref[2]).astype(o_ref.dtype)


def _gatefold(wq, wk, wv, wg):
    nh, dh, _ = wq.shape
    head = pl.BlockSpec((1, dh, dh), lambda h: (h, 0, 0))
    return pl.pallas_call(
        _gatefold_kernel,
        grid=(nh,),
        in_specs=[head, head, head, pl.BlockSpec((3, dh, V7X_LANES), lambda h: (0, h, 0))],
        out_specs=pl.BlockSpec((2, dh, V7X_LANES), lambda h: (0, h, 0)),
        out_shape=jax.ShapeDtypeStruct((2, nh * dh, V7X_LANES), BF16),
        compiler_params=_params(8 * dh * dh * 4, 1),
        name="gate_fold",
    )(wq, wk, wv, wg)


def _feat_kernel(xm_ref, xp_ref, xn_ref, cw_ref, cb_ref, wq_ref, wkt_ref, wv_ref, mg_ref, bg_ref,
                 tri_ref, sel_ref, q_ref, kt_ref, v_ref, xc_ref, g_ref, gt_ref, *, kscale):
    i = pl.program_id(1)
    last = pl.num_programs(1) - 1
    tm = xm_ref.shape[1]
    nh, dh = wq_ref.shape[0], wq_ref.shape[1]
    row = lax.broadcasted_iota(jnp.int32, (tm, dh), 0)
    gacc = jnp.zeros((tm, V7X_LANES), F32)
    for h in range(nh):
        cs = slice(h * dh, (h + 1) * dh)
        xb = xm_ref[0, :, cs]
        x = xb.astype(F32)
        prev = jnp.where(i > 0, xp_ref[0, HALO_ROWS - 1:HALO_ROWS, cs].astype(F32), 0.0)
        nxt = jnp.where(i < last, xn_ref[0, 0:1, cs].astype(F32), 0.0)
        x_m1 = jnp.where(row == 0, prev, pltpu.roll(x, 1, axis=0))
        x_p1 = jnp.where(row == tm - 1, nxt, pltpu.roll(x, tm - 1, axis=0))
        conv = cw_ref[0:1, cs] * x_m1 + cw_ref[1:2, cs] * x + cw_ref[2:3, cs] * x_p1 + cb_ref[:, cs]
        xc = _silu(conv).astype(BF16)
        xc_ref[0, :, cs] = xc
        q_ref[0, :, cs] = _dot(xc, wq_ref[h]).astype(BF16)
        kt_ref[0, cs, :] = (_dot_nt(wkt_ref[h], xc) * kscale).astype(BF16)
        v_ref[0, :, cs] = _dot(xb, wv_ref[h]).astype(BF16)
        gacc = gacc + _dot(xc, mg_ref[0, cs, :]) + _dot(xb, mg_ref[1, cs, :])
    g = gacc + bg_ref[...]
    hi, lo = _split_bf16(jax.nn.log_sigmoid(g))
    t = tri_ref.shape[1]
    sel = sel_ref[...]
    for c in range(tm // t):
        rs = slice(c * t, (c + 1) * t)
        fwd = _dot(tri_ref[0], hi[rs]) + _dot(tri_ref[0], lo[rs])
        bwd = _dot(tri_ref[1], hi[rs]) + _dot(tri_ref[1], lo[rs])
        cum = jnp.where(sel == 1.0, fwd, jnp.where(sel == 2.0, bwd, g[rs]))
        g_ref[0, rs, :] = cum
        gt_ref[0, :, rs] = cum.T[0:GATE_ROWS, :]


def _features(xmz, conv_w, conv_b, wq, wkt, wv, mg, bg, tri, sel, tm):
    bsz, seq, _ = xmz.shape
    nh, dh, _ = wq.shape
    e = nh * dh
    hb = tm // HALO_ROWS
    nhalo = seq // HALO_ROWS
    tok = pl.BlockSpec((1, tm, e), lambda b, i: (b, i, 0))
    t = tri.shape[1]
    nbytes = 2 * tm * e * 2 * 5 + 3 * e * dh * 2 + 2 * e * V7X_LANES * 2 + 10 * tm * dh * 4 + 2 * t * t * 2
    return pl.pallas_call(
        functools.partial(_feat_kernel, kscale=float(dh) ** -0.5),
        grid=(bsz, seq // tm),
        in_specs=[tok,
                  pl.BlockSpec((1, HALO_ROWS, e), lambda b, i: (b, jnp.maximum(i * hb - 1, 0), 0)),
                  pl.BlockSpec((1, HALO_ROWS, e), lambda b, i: (b, jnp.minimum((i + 1) * hb, nhalo - 1), 0)),
                  pl.BlockSpec((MLSTM_CONV_K, e), lambda b, i: (0, 0)),
                  pl.BlockSpec((1, e), lambda b, i: (0, 0)),
                  _resident((nh, dh, dh)), _resident((nh, dh, dh)), _resident((nh, dh, dh)),
                  _resident((2, e, V7X_LANES)),
                  pl.BlockSpec((1, V7X_LANES), lambda b, i: (0, 0)),
                  _resident((2, t, t)),
                  pl.BlockSpec((1, V7X_LANES), lambda b, i: (0, 0))],
        out_specs=[tok, pl.BlockSpec((1, e, tm), lambda b, i: (b, 0, i)), tok, tok,
                   pl.BlockSpec((1, tm, V7X_LANES), lambda b, i: (b, i, 0)),
                   pl.BlockSpec((1, GATE_ROWS, tm), lambda b, i: (b, 0, i))],
        out_shape=[jax.ShapeDtypeStruct((bsz, seq, e), BF16),
                   jax.ShapeDtypeStruct((bsz, e, seq), BF16),
                   jax.ShapeDtypeStruct((bsz, seq, e), BF16),
                   jax.ShapeDtypeStruct((bsz, seq, e), BF16),
                   jax.ShapeDtypeStruct((bsz, seq, V7X_LANES), F32),
                   jax.ShapeDtypeStruct((bsz, GATE_ROWS, seq), F32)],
        compiler_params=_params(nbytes, 2),
        name="mlstm_features",
    )(xmz, xmz, xmz, conv_w, conv_b.reshape(1, e), wq, wkt, wv, mg, bg, tri, sel)


def _scan_kernel(qf_ref, ktf_ref, vf_ref, gf_ref, gtf_ref, qb_ref, ktb_ref, vb_ref, gb_ref, gtb_ref,
                 ktx_ref, vx_ref, gtx_ref, mask_ref, o_ref, c_ref, n_ref, m_ref, *, nc, nh):
    h = pl.program_id(1)
    c = pl.program_id(2)
    t = qf_ref.shape[1]
    lane_g = lax.broadcasted_iota(jnp.int32, (1, V7X_LANES), 1)
    sub = lax.broadcasted_iota(jnp.int32, (V7X_BF16_SUBLANES, t), 0)

    def gate_rows(gt_ref, dd):
        ig_r = gt_ref[0, pl.ds(dd * nh + h, 1), :]
        b_r = gt_ref[0, pl.ds(2 * nh + dd * nh + h, 1), :]
        last = (t - 1) * (1 - dd)
        return ig_r, b_r, b_r[:, last:last + 1]

    def update_state(dd, kt, vb, ig_r, b_r, b_last):
        m = m_ref[dd]
        w_r = b_last - b_r + ig_r
        m_new = jnp.maximum(b_last + m, jnp.max(w_r, axis=1, keepdims=True))
        decay = jnp.exp(b_last + m - m_new)
        wexp = jnp.exp(w_r - m_new)
        kw = (kt.astype(F32) * wexp).astype(BF16)
        c_ref[dd] = decay * c_ref[dd] + _dot(kw, vb)
        w_hi = wexp.astype(BF16).astype(F32)
        w2 = jnp.where(sub == 0, w_hi, jnp.where(sub == 1, wexp - w_hi, 0.0)).astype(BF16)
        n_ref[dd] = decay * n_ref[dd] + jnp.sum(_dot_nt(w2, kt), axis=0, keepdims=True)
        m_ref[dd] = m_new

    @pl.when(c == 0)
    def _():
        c_ref[...] = jnp.zeros(c_ref.shape, F32)
        n_ref[...] = jnp.zeros(n_ref.shape, F32)
        m_ref[...] = jnp.zeros(m_ref.shape, F32)
        for dd in range(2):
            update_state(dd, ktx_ref[0], vx_ref[0], *gate_rows(gtx_ref, dd))

    streams = ((qf_ref, ktf_ref, vf_ref, gf_ref, gtf_ref), (qb_ref, ktb_ref, vb_ref, gb_ref, gtb_ref))
    outs, pending = [], []
    for dd, (q_ref, kt_ref, v_ref, g_ref, gt_ref) in enumerate(streams):
        qb, kt, vb = q_ref[0], kt_ref[0], v_ref[0]
        ig_r, b_r, b_last = gate_rows(gt_ref, dd)
        b_c = jnp.sum(jnp.where(lane_g == 2 * nh + dd * nh + h, g_ref[0], 0.0), axis=1, keepdims=True)
        m = m_ref[dd]
        dmat = jnp.where(mask_ref[dd] > 0.0, b_c - b_r + ig_r, -jnp.inf)
        inter = b_c + m
        m_t = jnp.maximum(inter, jnp.max(dmat, axis=1, keepdims=True))
        s = _dot(qb, kt) * jnp.exp(dmat - m_t)
        iw = jnp.exp(inter - m_t)
        num = iw * _dot(qb, c_ref[dd].astype(BF16)) + _dot(s.astype(BF16), vb)
        den = (iw * jnp.sum(qb.astype(F32) * n_ref[dd], axis=1, keepdims=True)
               + jnp.sum(s, axis=1, keepdims=True))
        outs.append(num * (1.0 / jnp.maximum(jnp.abs(den), jnp.exp(-m_t))))
        pending.append((dd, kt, vb, ig_r, b_r, b_last))

    rows_f = pl.ds(pl.multiple_of(c * t, t), t)
    rows_b = pl.ds(pl.multiple_of((nc - 1 - c) * t, t), t)

    @pl.when(c < nc // 2)
    def _():
        o_ref[0, rows_f, :] = outs[0]
        o_ref[0, rows_b, :] = outs[1]

    @pl.when(c >= nc // 2)
    def _():
        o_ref[0, rows_f, :] = o_ref[0, rows_f, :] + outs[0]
        o_ref[0, rows_b, :] = o_ref[0, rows_b, :] + outs[1]

    for args in pending:
        update_state(*args)


def _scan(q, kt, v, g, gt, ktx, vx, gtx, mask):
    bsz, seq, e = q.shape
    nh = MLSTM_HEADS
    dh = e // nh
    t = SCAN_CHUNK
    nc = seq // t
    assert nc % 2 == 0, "both directions advance together: needs an even number of chunks"
    assert vx.shape[1] == t, "context length must equal one scan chunk"
    fwd = lambda c: c
    bwd = lambda c: nc - 1 - c
    tok = lambda f: pl.BlockSpec((1, t, dh), lambda b, h, c: (b, f(c), h))
    ktb = lambda f: pl.BlockSpec((1, dh, t), lambda b, h, c: (b, h, f(c)))
    gcol = lambda f: pl.BlockSpec((1, t, V7X_LANES), lambda b, h, c: (b, f(c), 0))
    grow = lambda f: pl.BlockSpec((1, GATE_ROWS, t), lambda b, h, c: (b, 0, f(c)))
    zero = lambda c: 0
    nbytes = (2 * seq * dh * 4 + 16 * t * dh * 2 + 4 * dh * dh * 4 + 4 * t * V7X_LANES * 4
              + 2 * t * t * 4 + 24 * t * t * 4 + 8 * t * dh * 4)
    return pl.pallas_call(
        functools.partial(_scan_kernel, nc=nc, nh=nh),
        grid=(bsz, nh, nc),
        in_specs=[tok(fwd), ktb(fwd), tok(fwd), gcol(fwd), grow(fwd),
                  tok(bwd), ktb(bwd), tok(bwd), gcol(bwd), grow(bwd),
                  ktb(zero), tok(zero), grow(zero), _resident((2, t, t))],
        out_specs=pl.BlockSpec((1, seq, dh), lambda b, h, c: (b, 0, h)),
        out_shape=jax.ShapeDtypeStruct((bsz, seq, e), F32),
        scratch_shapes=[pltpu.VMEM((2, dh, dh), F32), pltpu.VMEM((2, 1, dh), F32),
                        pltpu.VMEM((2, 1, 1), F32)],
        compiler_params=_params(nbytes, 3),
        name="mlstm_scan",
    )(q, kt, v, g, gt, q, kt, v, g, gt, ktx, vx, gtx, mask)


def _mlout_kernel(hs_ref, xc_ref, sz_ref, h_ref, ng_ref, sk_ref, w_ref, gm_ref, lg_ref, lb_ref, o_ref,
                  *, alpha, nh):
    e = hs_ref.shape[2]
    dh = e // nh
    acc = jnp.zeros(h_ref.shape[1:], F32)
    for h in range(nh):
        cs = slice(h * dh, (h + 1) * dh)
        hh = hs_ref[0, :, cs]
        mu = jnp.mean(hh, axis=-1, keepdims=True)
        hc = hh - mu
        var = jnp.mean(hc * hc, axis=-1, keepdims=True)
        hn = (hc * (lax.rsqrt(var + LN_EPS) * ng_ref[:, cs])).astype(BF16)
        y = (hn + sk_ref[:, cs].astype(BF16) * xc_ref[0, :, cs]) * sz_ref[0, :, cs]
        acc = acc + _dot(y, w_ref[cs, :])
    o_ref[0] = _layer_norm(alpha * h_ref[0] + gm_ref[0] * acc, lg_ref[...], lb_ref[...])


def _mlout(hs, xc, xmz, h, norm_g, skip, w, gate, ln_g, ln_b, alpha, tm):
    bsz, seq, d = h.shape
    e = hs.shape[2]
    tok = lambda n, j=0: pl.BlockSpec((1, tm, n), lambda b, i: (b, i, j))
    rowe = pl.BlockSpec((1, e), lambda b, i: (0, 0))
    rowd = pl.BlockSpec((1, d), lambda b, i: (0, 0))
    nbytes = 2 * tm * e * 8 + 4 * tm * d * 4 + e * d * 2 + 6 * tm * (e // MLSTM_HEADS) * 4
    return pl.pallas_call(
        functools.partial(_mlout_kernel, alpha=alpha, nh=MLSTM_HEADS),
        grid=(bsz, seq // tm),
        in_specs=[tok(e), tok(e), tok(e, 1), tok(d), rowe, rowe, _resident((e, d)),
                  pl.BlockSpec((1, 1, d), lambda b, i: (b, 0, 0)), rowd, rowd],
        out_specs=tok(d),
        out_shape=jax.ShapeDtypeStruct((bsz, seq, d), F32),
        compiler_params=_params(nbytes, 2),
        name="mlstm_out_postnorm",
    )(hs, xc, xmz, h, norm_g.reshape(1, e), skip.reshape(1, e), w, gate, ln_g.reshape(1, d),
      ln_b.reshape(1, d))


def _pool_conv_layer(h, mods6, ab, ln_g, ln_b, alpha, grid2d, tm):
    sh_m, sc_m, g_m = mods6[0], mods6[1], mods6[2]
    w_in, pool_w, pool_b, pool_scale, conv_w, conv_b, norm_g, norm_b, w_out = ab
    seq = h.shape[1]
    pool_width = pool_w.shape[0] * POOL_GROUP
    p = _modmm(h, sh_m, sc_m, w_in, tm)
    band, inv = _pool_constants(seq, grid2d)
    ng = len(POOL_WINDOWS)
    ya = _pool_mixer(p, band, inv, pool_w, pool_b.reshape(ng, 1, POOL_GROUP),
                     pool_scale.reshape(ng, 1, POOL_GROUP), grid2d)
    yb = _conv_mixer(p, pool_width, conv_w, conv_b, norm_g, norm_b, tm)
    return _abproj(ya, yb, h, w_out, g_m, ln_g, ln_b, alpha, tm)


def kernel(x, c, ctx, c_ctx, mod_w, mod_b, ln_g, ln_b, ab_w_in, ab_pool_w, ab_pool_b, ab_pool_scale, ab_conv_w, ab_conv_b, ab_norm_g, ab_norm_b, ab_w_out, ml_w_in, ml_conv_w, ml_conv_b, ml_wq, ml_wk, ml_wv, ml_w_gate, ml_b_gate, ml_norm_g, ml_skip, ml_w_out, ffn_w_in, ffn_w_out):
    bsz, seq, d = x.shape
    depth = mod_w.shape[0]
    nh = MLSTM_HEADS
    alpha = (2.0 * depth) ** 0.25
    tm_x, tm_c = 512, ctx.shape[1]

    n_rows = -(-(bsz + 1) // V7X_SUBLANES) * V7X_SUBLANES
    rows = jnp.concatenate([c, c_ctx[None, :], jnp.zeros((n_rows - bsz - 1, d), F32)], axis=0)
    mods = _mods(rows, mod_w, mod_b)

    h, hc = x, ctx
    for l in range(depth):
        last = l == depth - 1
        even = l % 2 == 0
        j = l // 2
        mods_x = [m.reshape(bsz, 1, d) for m in jnp.split(mods[l, :bsz], 6, axis=-1)]
        mods_c = [jnp.broadcast_to(m.reshape(1, 1, d), (bsz, 1, d)) for m in jnp.split(mods[l, bsz], 6)]
        ffn_wi, ffn_wo = ffn_w_in[l].astype(BF16), ffn_w_out[l].astype(BF16)
        if even:
            ab = (ab_w_in[j].astype(BF16), ab_pool_w[j].astype(BF16), ab_pool_b[j], ab_pool_scale[j],
                  ab_conv_w[j], ab_conv_b[j], ab_norm_g[j], ab_norm_b[j], ab_w_out[j].astype(BF16))
            h1 = _pool_conv_layer(h, mods_x, ab, ln_g[l, 0], ln_b[l, 0], alpha, True, tm_x)
            if not last:
                hc1 = _pool_conv_layer(hc, mods_c, ab, ln_g[l, 0], ln_b[l, 0], alpha, False, tm_c)
        else:
            e = ml_w_in.shape[2] // 2
            w_in = ml_w_in[j].astype(BF16)
            wq, wv = ml_wq[j].astype(BF16), ml_wv[j].astype(BF16)
            wkt = jnp.swapaxes(ml_wk[j], 1, 2).astype(BF16)
            wg = jnp.transpose(ml_w_gate[j].reshape(2, 3, e, 2, nh), (1, 2, 3, 0, 4)).reshape(3, e, GATE_ROWS)
            wg = jnp.pad(wg, ((0, 0), (0, 0), (0, V7X_LANES - GATE_ROWS)))
            mg = _gatefold(ml_wq[j], ml_wk[j], ml_wv[j], wg)
            bg = jnp.transpose(ml_b_gate[j].reshape(2, 2, nh), (1, 0, 2)).reshape(1, GATE_ROWS)
            bg = jnp.pad(bg, ((0, 0), (0, V7X_LANES - GATE_ROWS)))
            t = SCAN_CHUNK
            idx = jnp.arange(t)
            lower = (idx[None, :] <= idx[:, None])
            tri = jnp.stack([lower, lower.T]).astype(BF16)
            mask = jnp.stack([lower, lower.T]).astype(F32)
            col = jnp.arange(V7X_LANES)
            sel = jnp.where((col >= 2 * nh) & (col < 3 * nh), 1.0,
                            jnp.where((col >= 3 * nh) & (col < 4 * nh), 2.0, 0.0)).reshape(1, V7X_LANES)
            feat = (ml_conv_w[j], ml_conv_b[j], wq, wkt, wv, mg, bg, tri, sel)
            xm_c = _modmm(hc, mods_c[0], mods_c[1], w_in if not last else w_in[:, :e], tm_c,
                          silu_from=e)
            _, kt_c, v_c, _, _, gt_c = _features(xm_c, *feat, tm_c)
            xmz = _modmm(h, mods_x[0], mods_x[1], w_in, tm_x, silu_from=e)
            q, kt, v, xc, g, gt = _features(xmz, *feat, tm_x)
            hs = _scan(q, kt, v, g, gt, kt_c, v_c, gt_c, mask)
            w_out = ml_w_out[j].astype(BF16)
            h1 = _mlout(hs, xc, xmz, h, ml_norm_g[j], ml_skip[j], w_out, mods_x[2], ln_g[l, 0],
                        ln_b[l, 0], alpha, tm_x)
            if not last:
                raise NotImplementedError("context output of an mLSTM layer is only needed for depth > 2")
        h = _ffn(h1, mods_x[3], mods_x[4], mods_x[5], ffn_wi, ffn_wo, ln_g[l, 1], ln_b[l, 1], alpha, tm_x)
        if not last:
            hc = _ffn(hc1, mods_c[3], mods_c[4], mods_c[5], ffn_wi, ffn_wo, ln_g[l, 1], ln_b[l, 1],
                      alpha, tm_c)
    return h
```

```python
import functools

import jax
import jax.numpy as jnp
from jax import lax
from jax.experimental import pallas as pl
from jax.experimental.pallas import tpu as pltpu

F32 = jnp.float32
BF16 = jnp.bfloat16

GRID_W = 64
POOL_WINDOWS = (2, 4, 8, 16)
POOL_GROUP = 128
CONV_K = 31
MLSTM_HEADS = 4
MLSTM_CONV_K = 3
LN_EPS = 1e-5

V7X_VMEM_BYTES = 64 * 1024 * 1024
V7X_LANES = 128
V7X_SUBLANES = 8
V7X_BF16_SUBLANES = 16
V7X_MXU_DIM = 256

SCAN_CHUNK = V7X_MXU_DIM
HALO_ROWS = V7X_BF16_SUBLANES
POOL_PAD_ROWS = (max(POOL_WINDOWS) // 2) * GRID_W
GATE_ROWS = 4 * MLSTM_HEADS
FFN_SUB_ROWS = 2 * V7X_MXU_DIM


def _vmem_limit(nbytes):
    return int(min(nbytes * 5 // 4 + (4 << 20), V7X_VMEM_BYTES - (6 << 20)))


def _params(nbytes, n_axes):
    return pltpu.CompilerParams(dimension_semantics=("arbitrary",) * n_axes,
                                vmem_limit_bytes=_vmem_limit(nbytes))


def _resident(shape):
    nd = len(shape)
    return pl.BlockSpec(shape, lambda *_: (0,) * nd, pipeline_mode=pl.Buffered(1))


def _silu(x):
    return x * jax.nn.sigmoid(x)


def _layer_norm(r, g, b):
    mu = jnp.mean(r, axis=-1, keepdims=True)
    xc = r - mu
    var = jnp.mean(xc * xc, axis=-1, keepdims=True)
    return xc * lax.rsqrt(var + LN_EPS) * g + b


def _dot(a, b):
    return jnp.dot(a, b, preferred_element_type=F32)


def _dot_nt(a, b):
    return lax.dot_general(a, b, (((1,), (1,)), ((), ())), preferred_element_type=F32)


def _split_bf16(x):
    hi = x.astype(BF16)
    return hi, (x - hi.astype(F32)).astype(BF16)


def _mods_kernel(x_ref, w_ref, b_ref, o_ref):
    s = _silu(x_ref[...]).astype(BF16)
    o_ref[0] = _dot(s, w_ref[0].astype(BF16)) + b_ref[0]


def _mods(rows, mod_w, mod_b):
    depth, d, n = mod_w.shape
    r = rows.shape[0]
    tn = n // 4
    return pl.pallas_call(
        _mods_kernel,
        grid=(depth, n // tn),
        in_specs=[pl.BlockSpec((r, d), lambda l, j: (0, 0)),
                  pl.BlockSpec((1, d, tn), lambda l, j: (l, 0, j)),
                  pl.BlockSpec((1, 1, tn), lambda l, j: (l, 0, j))],
        out_specs=pl.BlockSpec((1, r, tn), lambda l, j: (l, 0, j)),
        out_shape=jax.ShapeDtypeStruct((depth, r, n), F32),
        compiler_params=_params(2 * d * tn * 4 + d * tn * 2, 2),
        name="mods",
    )(rows, mod_w, mod_b.reshape(depth, 1, n))


def _modmm_kernel(x_ref, md_ref, w_ref, o_ref, *, n_chunk, silu_from):
    u = (x_ref[0] * (1.0 + md_ref[0, 1:2, :]) + md_ref[0, 0:1, :]).astype(BF16)
    for j in range(0, w_ref.shape[1], n_chunk):
        y = _dot(u, w_ref[:, j:j + n_chunk])
        if j >= silu_from:
            y = _silu(y)
        o_ref[0, :, j:j + n_chunk] = y.astype(o_ref.dtype)


def _modmm(x, md, w, tm, silu_from=None):
    bsz, seq, d = x.shape
    n = w.shape[1]
    silu_from = n if silu_from is None else silu_from
    nbytes = 2 * tm * d * 4 + d * n * 2 + 2 * tm * n * 2 + tm * d * 2
    return pl.pallas_call(
        functools.partial(_modmm_kernel, n_chunk=512, silu_from=silu_from),
        grid=(bsz, seq // tm),
        in_specs=[pl.BlockSpec((1, tm, d), lambda b, i: (b, i, 0)),
                  pl.BlockSpec((1,) + md.shape[1:], lambda b, i: (b, 0, 0)), _resident((d, n))],
        out_specs=pl.BlockSpec((1, tm, n), lambda b, i: (b, i, 0)),
        out_shape=jax.ShapeDtypeStruct((bsz, seq, n), BF16),
        compiler_params=_params(nbytes, 2),
        name="modmm",
    )(x, md, w)


def _pool_kernel(a_ref, band_ref, inv_ref, pw_ref, pb_ref, ps_ref, o_ref, pad_ref, *, grid2d):
    g = pl.program_id(1)
    seq = a_ref.shape[1]
    blk = band_ref.shape[1]
    band = band_ref[0]
    for i in range(seq // blk):
        pad_ref[pl.ds(POOL_PAD_ROWS + i * blk, blk), :] = _dot(band, a_ref[0, pl.ds(i * blk, blk), :])
    if grid2d:
        zeros = jnp.zeros((POOL_PAD_ROWS, POOL_GROUP), F32)
        pad_ref[pl.ds(0, POOL_PAD_ROWS), :] = zeros
        pad_ref[pl.ds(POOL_PAD_ROWS + seq, POOL_PAD_ROWS), :] = zeros

    ch = 256

    def finish(i, summed):
        rows = pl.ds(pl.multiple_of(i * ch, ch), ch)
        seg = a_ref[0, rows, :].astype(F32)
        d = (summed * inv_ref[0, rows, :] - seg).astype(BF16)
        y = (_dot(d, pw_ref[0]) + pb_ref[0]) * ps_ref[0]
        o_ref[0, rows, :] = y.astype(o_ref.dtype)

    if not grid2d:
        def body(i, carry):
            finish(i, pad_ref[pl.ds(pl.multiple_of(POOL_PAD_ROWS + i * ch, ch), ch), :])
            return carry
        lax.fori_loop(0, seq // ch, body, 0)
        return

    for gi, w in enumerate(POOL_WINDOWS):
        @pl.when(g == gi)
        def _(w=w):
            def body(i, carry):
                base = POOL_PAD_ROWS + i * ch
                acc = jnp.zeros((ch, POOL_GROUP), F32)
                for k in range(-(w // 2), w - w // 2):
                    acc = acc + pad_ref[pl.ds(pl.multiple_of(base + k * GRID_W, GRID_W), ch), :]
                finish(i, acc)
                return carry
            lax.fori_loop(0, seq // ch, body, 0)


def _pool_mixer(p, band, inv, pool_w, pool_b, pool_scale, grid2d):
    bsz, seq, _ = p.shape
    ng = len(POOL_WINDOWS)
    blk = band.shape[1]
    grp = lambda b, g: (g, 0, 0)
    nbytes = 4 * seq * POOL_GROUP * 2 + 2 * seq * V7X_LANES * 4 + (seq + 2 * POOL_PAD_ROWS) * POOL_GROUP * 4
    return pl.pallas_call(
        functools.partial(_pool_kernel, grid2d=grid2d),
        grid=(bsz, ng),
        in_specs=[pl.BlockSpec((1, seq, POOL_GROUP), lambda b, g: (b, 0, g)),
                  pl.BlockSpec((1, blk, blk), grp),
                  pl.BlockSpec((1, seq, 1), grp),
                  pl.BlockSpec((1, POOL_GROUP, POOL_GROUP), grp),
                  pl.BlockSpec((1, 1, POOL_GROUP), grp),
                  pl.BlockSpec((1, 1, POOL_GROUP), grp)],
        out_specs=pl.BlockSpec((1, seq, POOL_GROUP), lambda b, g: (b, 0, g)),
        out_shape=jax.ShapeDtypeStruct((bsz, seq, ng * POOL_GROUP), BF16),
        scratch_shapes=[pltpu.VMEM((seq + 2 * POOL_PAD_ROWS, POOL_GROUP), F32)],
        compiler_params=_params(nbytes, 2),
        name="pool_mixer",
    )(p, band, inv, pool_w, pool_b, pool_scale)


def _pool_constants(seq, grid2d):
    def band1d(n, w):
        t = jnp.arange(n)
        lo = jnp.clip(t - w // 2, 0, n)
        hi = jnp.clip(t + w - w // 2, 0, n)
        s = t[None, :]
        return ((s >= lo[:, None]) & (s < hi[:, None])).astype(F32), (hi - lo).astype(F32)

    bands, invs = [], []
    for w in POOL_WINDOWS:
        if grid2d:
            m, cnt = band1d(GRID_W, w)
            rows = seq // GRID_W
            _, rcnt = band1d(rows, w)
            bands.append(jnp.kron(jnp.eye(V7X_MXU_DIM // GRID_W, dtype=F32), m))
            invs.append(1.0 / (rcnt[:, None] * cnt[None, :]).reshape(seq))
        else:
            m, cnt = band1d(seq, w)
            bands.append(m)
            invs.append(1.0 / cnt)
    return jnp.stack(bands).astype(BF16), jnp.stack(invs).reshape(len(POOL_WINDOWS), seq, 1)


def _conv_kernel(v_ref, g_ref, vp_ref, gp_ref, vn_ref, gn_ref, cw_ref, cb_ref, ng_ref, nb_ref,
                 o_ref, x_ref, y_ref):
    i = pl.program_id(1)
    last = pl.num_programs(1) - 1
    tt, cw = v_ref.shape[1], v_ref.shape[2]

    def glu(v, g):
        return v.astype(F32) * jax.nn.sigmoid(g.astype(F32))

    x_ref[pl.ds(HALO_ROWS, tt), :] = glu(v_ref[0], g_ref[0])
    x_ref[pl.ds(0, HALO_ROWS), :] = jnp.where(i > 0, glu(vp_ref[0], gp_ref[0]), 0.0)
    x_ref[pl.ds(HALO_ROWS + tt, HALO_ROWS), :] = jnp.where(i < last, glu(vn_ref[0], gn_ref[0]), 0.0)

    ct, cl = 128, V7X_LANES
    n_a = (CONV_K + 1) // V7X_SUBLANES
    for c0 in range(0, cw, cl):
        def body(j, carry, c0=c0):
            base = pl.multiple_of(j * ct, ct)
            out = jnp.zeros((ct, cl), F32)
            for r in range(V7X_SUBLANES):
                part = jnp.zeros((ct + V7X_SUBLANES, cl), F32)
                for a in range(n_a):
                    k = V7X_SUBLANES * a + r - 1
                    if 0 <= k < CONV_K:
                        win = x_ref[pl.ds(base + V7X_SUBLANES * a, ct + V7X_SUBLANES), pl.ds(c0, cl)]
                        part = part + cw_ref[k:k + 1, c0:c0 + cl] * win
                out = out + part[r:r + ct, :]
            y_ref[pl.ds(base, ct), pl.ds(c0, cl)] = out
            return carry
        lax.fori_loop(0, tt // ct, body, 0)

    y = y_ref[...] + cb_ref[...]
    o_ref[0] = _silu(_layer_norm(y, ng_ref[...], nb_ref[...])).astype(o_ref.dtype)


def _conv_mixer(p, pool_width, conv_w, conv_b, norm_g, norm_b, tt):
    bsz, seq, _ = p.shape
    k, cw = conv_w.shape
    nb = pool_width // cw
    hb = tt // HALO_ROWS
    nhalo = seq // HALO_ROWS
    cur = lambda j: pl.BlockSpec((1, tt, cw), lambda b, i: (b, i, j))
    prv = lambda j: pl.BlockSpec((1, HALO_ROWS, cw), lambda b, i: (b, jnp.maximum(i * hb - 1, 0), j))
    nxt = lambda j: pl.BlockSpec((1, HALO_ROWS, cw), lambda b, i: (b, jnp.minimum((i + 1) * hb, nhalo - 1), j))
    row = lambda n: pl.BlockSpec((n, cw), lambda b, i: (0, 0))
    nbytes = 4 * tt * cw * 2 + 2 * tt * cw * 2 + (2 * tt + 2 * HALO_ROWS) * cw * 4 + 4 * tt * cw * 4
    return pl.pallas_call(
        _conv_kernel,
        grid=(bsz, seq // tt),
        in_specs=[cur(nb), cur(nb + 1), prv(nb), prv(nb + 1), nxt(nb), nxt(nb + 1),
                  row(k), row(1), row(1), row(1)],
        out_specs=pl.BlockSpec((1, tt, cw), lambda b, i: (b, i, 0)),
        out_shape=jax.ShapeDtypeStruct((bsz, seq, cw), BF16),
        scratch_shapes=[pltpu.VMEM((tt + 2 * HALO_ROWS, cw), F32), pltpu.VMEM((tt, cw), F32)],
        compiler_params=_params(nbytes, 2),
        name="conv_mixer",
    )(p, p, p, p, p, p, conv_w, conv_b.reshape(1, cw), norm_g.reshape(1, cw), norm_b.reshape(1, cw))


def _ffn_rows(h1, md_ref, wi_ref, wo_ref, ln_ref, alpha, fc):
    u = (h1 * (1.0 + md_ref[0, 4:5, :]) + md_ref[0, 3:4, :]).astype(BF16)
    f = wo_ref.shape[0]
    acc = jnp.zeros(h1.shape, F32)
    for c in range(0, f, fc):
        a = _dot(u, wi_ref[:, c:c + fc])
        g = _dot(u, wi_ref[:, f + c:f + c + fc])
        acc = acc + _dot((_silu(a) * g).astype(BF16), wo_ref[c:c + fc, :])
    return _layer_norm(alpha * h1 + md_ref[0, 5:6, :] * acc, ln_ref[2:3, :], ln_ref[3:4, :])


def _ab_ffn_kernel(ya_ref, yb_ref, h_ref, wab_ref, md_ref, wi_ref, wo_ref, ln_ref, o_ref,
                   *, alpha, fc, sub):
    ka = ya_ref.shape[2]
    for r0 in range(0, h_ref.shape[1], sub):
        rs = slice(r0, r0 + sub)
        y = _dot(ya_ref[0, rs, :], wab_ref[0:ka, :]) + _dot(yb_ref[0, rs, :], wab_ref[ka:, :])
        h1 = _layer_norm(alpha * h_ref[0, rs, :] + md_ref[0, 2:3, :] * y, ln_ref[0:1, :], ln_ref[1:2, :])
        o_ref[0, rs, :] = _ffn_rows(h1, md_ref, wi_ref, wo_ref, ln_ref, alpha, fc)


def _ab_ffn(ya, yb, h, w_ab, md, w_in, w_out, ln, alpha, tm):
    bsz, seq, d = h.shape
    ka, kb = ya.shape[2], yb.shape[2]
    f = w_out.shape[0]
    tok = lambda n: pl.BlockSpec((1, tm, n), lambda b, i: (b, i, 0))
    nbytes = (2 * tm * (ka + kb) * 2 + 4 * tm * d * 4 + (ka + kb) * d * 2 + 3 * d * f * 2
              + 6 * tm * d * 4)
    return pl.pallas_call(
        functools.partial(_ab_ffn_kernel, alpha=alpha, fc=V7X_MXU_DIM, sub=min(tm, FFN_SUB_ROWS)),
        grid=(bsz, seq // tm),
        in_specs=[tok(ka), tok(kb), tok(d), _resident((ka + kb, d)),
                  pl.BlockSpec((1,) + md.shape[1:], lambda b, i: (b, 0, 0)),
                  _resident((d, 2 * f)), _resident((f, d)),
                  pl.BlockSpec(ln.shape, lambda b, i: (0, 0))],
        out_specs=tok(d),
        out_shape=jax.ShapeDtypeStruct((bsz, seq, d), F32),
        compiler_params=_params(nbytes, 2),
        name="abproj_ffn",
    )(ya, yb, h, w_ab, md, w_in, w_out, ln)


def _gatefold_kernel(wq_ref, wk_ref, wv_ref, wg_ref, o_ref):
    def mm(a, b):
        ah, al = _split_bf16(a)
        bh, bl = _split_bf16(b)
        return _dot(ah, bh) + _dot(ah, bl) + _dot(al, bh)

    o_ref[0] = (mm(wq_ref[0], wg_ref[0]) + mm(wk_ref[0], wg_ref[1])).astype(o_ref.dtype)
    o_ref[1] = mm(wv_ref[0], wg_ref[2]).astype(o_ref.dtype)


def _gatefold(wq, wk, wv, wg):
    nh, dh, _ = wq.shape
    head = pl.BlockSpec((1, dh, dh), lambda h: (h, 0, 0))
    return pl.pallas_call(
        _gatefold_kernel,
        grid=(nh,),
        in_specs=[head, head, head, pl.BlockSpec((3, dh, V7X_LANES), lambda h: (0, h, 0))],
        out_specs=pl.BlockSpec((2, dh, V7X_LANES), lambda h: (0, h, 0)),
        out_shape=jax.ShapeDtypeStruct((2, nh * dh, V7X_LANES), BF16),
        compiler_params=_params(8 * dh * dh * 4, 1),
        name="gate_fold",
    )(wq, wk, wv, wg)


def _feat_kernel(xm_ref, xp_ref, xn_ref, cw_ref, cb_ref, wq_ref, wkt_ref, wv_ref, mg_ref, bg_ref,
                 tri_ref, sel_ref, q_ref, kt_ref, v_ref, xc_ref, g_ref, gt_ref, *, kscale):
    i = pl.program_id(1)
    last = pl.num_programs(1) - 1
    tm = xm_ref.shape[1]
    nh, dh = wq_ref.shape[0], wq_ref.shape[1]
    row = lax.broadcasted_iota(jnp.int32, (tm, dh), 0)
    gacc = jnp.zeros((tm, V7X_LANES), F32)
    for h in range(nh):
        cs = slice(h * dh, (h + 1) * dh)
        xb = xm_ref[0, :, cs]
        x = xb.astype(F32)
        prev = jnp.where(i > 0, xp_ref[0, HALO_ROWS - 1:HALO_ROWS, cs].astype(F32), 0.0)
        nxt = jnp.where(i < last, xn_ref[0, 0:1, cs].astype(F32), 0.0)
        x_m1 = jnp.where(row == 0, prev, pltpu.roll(x, 1, axis=0))
        x_p1 = jnp.where(row == tm - 1, nxt, pltpu.roll(x, tm - 1, axis=0))
        conv = cw_ref[0:1, cs] * x_m1 + cw_ref[1:2, cs] * x + cw_ref[2:3, cs] * x_p1 + cb_ref[:, cs]
        xc = _silu(conv).astype(BF16)
        xc_ref[0, :, cs] = xc
        q_ref[0, h] = _dot(xc, wq_ref[h]).astype(BF16)
        kt = (_dot_nt(wkt_ref[h], xc) * kscale).astype(BF16)
        for c in range(tm // tri_ref.shape[1]):
            kt_ref[0, h, c] = kt[:, c * tri_ref.shape[1]:(c + 1) * tri_ref.shape[1]]
        v_ref[0, h] = _dot(xb, wv_ref[h]).astype(BF16)
        gacc = gacc + _dot(xc, mg_ref[0, cs, :]) + _dot(xb, mg_ref[1, cs, :])
    g = gacc + bg_ref[...]
    hi, lo = _split_bf16(jax.nn.log_sigmoid(g))
    t = tri_ref.shape[1]
    sel = sel_ref[...]
    for c in range(tm // t):
        rs = slice(c * t, (c + 1) * t)
        fwd = _dot(tri_ref[0], hi[rs]) + _dot(tri_ref[0], lo[rs])
        bwd = _dot(tri_ref[1], hi[rs]) + _dot(tri_ref[1], lo[rs])
        cum = jnp.where(sel == 1.0, fwd, jnp.where(sel == 2.0, bwd, g[rs]))
        g_ref[0, rs, :] = cum
        gt_ref[0, c] = cum.T[0:GATE_ROWS, :]


def _features(xmz, conv_w, conv_b, wq, wkt, wv, mg, bg, tri, sel, tm):
    bsz, seq, _ = xmz.shape
    nh, dh, _ = wq.shape
    e = nh * dh
    hb = tm // HALO_ROWS
    nhalo = seq // HALO_ROWS
    tok = pl.BlockSpec((1, tm, e), lambda b, i: (b, i, 0))
    t = tri.shape[1]
    nbytes = 2 * tm * e * 2 * 5 + 3 * e * dh * 2 + 2 * e * V7X_LANES * 2 + 10 * tm * dh * 4 + 2 * t * t * 2
    return pl.pallas_call(
        functools.partial(_feat_kernel, kscale=float(dh) ** -0.5),
        grid=(bsz, seq // tm),
        in_specs=[tok,
                  pl.BlockSpec((1, HALO_ROWS, e), lambda b, i: (b, jnp.maximum(i * hb - 1, 0), 0)),
                  pl.BlockSpec((1, HALO_ROWS, e), lambda b, i: (b, jnp.minimum((i + 1) * hb, nhalo - 1), 0)),
                  pl.BlockSpec((MLSTM_CONV_K, e), lambda b, i: (0, 0)),
                  pl.BlockSpec((1, e), lambda b, i: (0, 0)),
                  _resident((nh, dh, dh)), _resident((nh, dh, dh)), _resident((nh, dh, dh)),
                  _resident((2, e, V7X_LANES)),
                  pl.BlockSpec((1, V7X_LANES), lambda b, i: (0, 0)),
                  _resident((2, t, t)),
                  pl.BlockSpec((1, V7X_LANES), lambda b, i: (0, 0))],
        out_specs=[pl.BlockSpec((1, nh, tm, dh), lambda b, i: (b, 0, i, 0)),
                   pl.BlockSpec((1, nh, tm // t, dh, t), lambda b, i: (b, 0, i, 0, 0)),
                   pl.BlockSpec((1, nh, tm, dh), lambda b, i: (b, 0, i, 0)),
                   tok,
                   pl.BlockSpec((1, tm, V7X_LANES), lambda b, i: (b, i, 0)),
                   pl.BlockSpec((1, tm // t, GATE_ROWS, t), lambda b, i: (b, i, 0, 0))],
        out_shape=[jax.ShapeDtypeStruct((bsz, nh, seq, dh), BF16),
                   jax.ShapeDtypeStruct((bsz, nh, seq // t, dh, t), BF16),
                   jax.ShapeDtypeStruct((bsz, nh, seq, dh), BF16),
                   jax.ShapeDtypeStruct((bsz, seq, e), BF16),
                   jax.ShapeDtypeStruct((bsz, seq, V7X_LANES), F32),
                   jax.ShapeDtypeStruct((bsz, seq // t, GATE_ROWS, t), F32)],
        compiler_params=_params(nbytes, 2),
        name="mlstm_features",
    )(xmz, xmz, xmz, conv_w, conv_b.reshape(1, e), wq, wkt, wv, mg, bg, tri, sel)


def _scan_kernel(qf_ref, ktf_ref, vf_ref, gf_ref, gtf_ref, qb_ref, ktb_ref, vb_ref, gb_ref, gtb_ref,
                 ktx_ref, vx_ref, gtx_ref, mask_ref, o_ref, c_ref, n_ref, m_ref, *, nc, nh):
    h = pl.program_id(1)
    c = pl.program_id(2)
    t = qf_ref.shape[2]
    lane_g = lax.broadcasted_iota(jnp.int32, (1, V7X_LANES), 1)
    sub = lax.broadcasted_iota(jnp.int32, (V7X_BF16_SUBLANES, t), 0)

    def gate_rows(gt_ref, dd):
        ig_r = gt_ref[0, 0, pl.ds(dd * nh + h, 1), :]
        b_r = gt_ref[0, 0, pl.ds(2 * nh + dd * nh + h, 1), :]
        last = (t - 1) * (1 - dd)
        return ig_r, b_r, b_r[:, last:last + 1]

    def update_state(dd, kt, vb, ig_r, b_r, b_last):
        m = m_ref[dd]
        w_r = b_last - b_r + ig_r
        m_new = jnp.maximum(b_last + m, jnp.max(w_r, axis=1, keepdims=True))
        decay = jnp.exp(b_last + m - m_new)
        wexp = jnp.exp(w_r - m_new)
        kw = kt * wexp.astype(BF16)
        c_ref[dd] = decay * c_ref[dd] + _dot(kw, vb)
        w_hi = wexp.astype(BF16).astype(F32)
        w2 = jnp.where(sub == 0, w_hi, jnp.where(sub == 1, wexp - w_hi, 0.0)).astype(BF16)
        n_ref[dd] = decay * n_ref[dd] + jnp.sum(_dot_nt(w2, kt), axis=0, keepdims=True)
        m_ref[dd] = m_new

    @pl.when(c == 0)
    def _():
        c_ref[...] = jnp.zeros(c_ref.shape, F32)
        n_ref[...] = jnp.zeros(n_ref.shape, F32)
        m_ref[...] = jnp.zeros(m_ref.shape, F32)
        o_ref[...] = jnp.zeros(o_ref.shape, o_ref.dtype)
        for dd in range(2):
            update_state(dd, ktx_ref[0, 0, 0], vx_ref[0, 0], *gate_rows(gtx_ref, dd))

    streams = ((qf_ref, ktf_ref, vf_ref, gf_ref, gtf_ref), (qb_ref, ktb_ref, vb_ref, gb_ref, gtb_ref))
    outs, pending = [], []
    for dd, (q_ref, kt_ref, v_ref, g_ref, gt_ref) in enumerate(streams):
        qb, kt, vb = q_ref[0, 0], kt_ref[0, 0, 0], v_ref[0, 0]
        ig_r, b_r, b_last = gate_rows(gt_ref, dd)
        b_c = jnp.sum(jnp.where(lane_g == 2 * nh + dd * nh + h, g_ref[0], 0.0), axis=1, keepdims=True)
        m = m_ref[dd]
        dmat = jnp.where(mask_ref[dd] > 0.0, b_c - b_r + ig_r, -jnp.inf)
        inter = b_c + m
        m_t = jnp.maximum(inter, jnp.max(dmat, axis=1, keepdims=True))
        s = _dot(qb, kt) * jnp.exp(dmat - m_t)
        iw = jnp.exp(inter - m_t)
        num = iw * _dot(qb, c_ref[dd].astype(BF16)) + _dot(s.astype(BF16), vb)
        den = (iw * jnp.sum(qb.astype(F32) * n_ref[dd], axis=1, keepdims=True)
               + jnp.sum(s, axis=1, keepdims=True))
        outs.append(num * (1.0 / jnp.maximum(jnp.abs(den), jnp.exp(-m_t))))
        pending.append((dd, kt, vb, ig_r, b_r, b_last))

    for rows, hout in ((pl.ds(pl.multiple_of(c * t, t), t), outs[0]),
                       (pl.ds(pl.multiple_of((nc - 1 - c) * t, t), t), outs[1])):
        o_ref[0, rows, :] = (o_ref[0, rows, :].astype(F32) + hout).astype(o_ref.dtype)

    for args in pending:
        update_state(*args)


def _scan(q, kt, v, g, gt, ktx, vx, gtx, mask):
    bsz, nh, seq, dh = q.shape
    t = SCAN_CHUNK
    nc = seq // t
    assert nc % 2 == 0, "both directions advance together: needs an even number of chunks"
    assert vx.shape[2] == t, "context length must equal one scan chunk"
    fwd = lambda c: c
    bwd = lambda c: nc - 1 - c
    zero = lambda c: 0
    tok = lambda f: pl.BlockSpec((1, 1, t, dh), lambda b, h, c: (b, h, f(c), 0))
    ktb = lambda f: pl.BlockSpec((1, 1, 1, dh, t), lambda b, h, c: (b, h, f(c), 0, 0))
    gcol = lambda f: pl.BlockSpec((1, t, V7X_LANES), lambda b, h, c: (b, f(c), 0))
    grow = lambda f: pl.BlockSpec((1, 1, GATE_ROWS, t), lambda b, h, c: (b, f(c), 0, 0))
    nbytes = (2 * seq * dh * 2 + 16 * t * dh * 2 + 4 * dh * dh * 4 + 4 * t * V7X_LANES * 4
              + 2 * t * t * 4 + 24 * t * t * 4 + 8 * t * dh * 4)
    return pl.pallas_call(
        functools.partial(_scan_kernel, nc=nc, nh=nh),
        grid=(bsz, nh, nc),
        in_specs=[tok(fwd), ktb(fwd), tok(fwd), gcol(fwd), grow(fwd),
                  tok(bwd), ktb(bwd), tok(bwd), gcol(bwd), grow(bwd),
                  ktb(zero), tok(zero), grow(zero), _resident((2, t, t))],
        out_specs=pl.BlockSpec((1, seq, dh), lambda b, h, c: (b, 0, h)),
        out_shape=jax.ShapeDtypeStruct((bsz, seq, nh * dh), BF16),
        scratch_shapes=[pltpu.VMEM((2, dh, dh), F32), pltpu.VMEM((2, 1, dh), F32),
                        pltpu.VMEM((2, 1, 1), F32)],
        compiler_params=_params(nbytes, 3),
        name="mlstm_scan",
    )(q, kt, v, g, gt, q, kt, v, g, gt, ktx, vx, gtx, mask)


def _ml_ffn_kernel(hs_ref, xc_ref, sz_ref, h_ref, ng_ref, sk_ref, wml_ref, md_ref, wi_ref, wo_ref,
                   ln_ref, o_ref, *, alpha, fc, sub, nh):
    e = hs_ref.shape[2]
    dh = e // nh
    for r0 in range(0, h_ref.shape[1], sub):
        rs = slice(r0, r0 + sub)
        acc = jnp.zeros((sub, h_ref.shape[2]), F32)
        for h in range(nh):
            cs = slice(h * dh, (h + 1) * dh)
            hh = hs_ref[0, rs, cs].astype(F32)
            mu = jnp.mean(hh, axis=-1, keepdims=True)
            hc = hh - mu
            var = jnp.mean(hc * hc, axis=-1, keepdims=True)
            hn = (hc * (lax.rsqrt(var + LN_EPS) * ng_ref[:, cs])).astype(BF16)
            y = (hn + sk_ref[:, cs].astype(BF16) * xc_ref[0, rs, cs]) * sz_ref[0, rs, cs]
            acc = acc + _dot(y, wml_ref[cs, :])
        h1 = _layer_norm(alpha * h_ref[0, rs, :] + md_ref[0, 2:3, :] * acc, ln_ref[0:1, :], ln_ref[1:2, :])
        o_ref[0, rs, :] = _ffn_rows(h1, md_ref, wi_ref, wo_ref, ln_ref, alpha, fc)


def _ml_ffn(hs, xc, xmz, h, norm_g, skip, w_ml, md, w_in, w_out, ln, alpha, tm):
    bsz, seq, d = h.shape
    e = hs.shape[2]
    f = w_out.shape[0]
    tok = lambda n, j=0: pl.BlockSpec((1, tm, n), lambda b, i: (b, i, j))
    rowe = pl.BlockSpec((1, e), lambda b, i: (0, 0))
    nbytes = 2 * tm * e * 6 + 4 * tm * d * 4 + e * d * 2 + 3 * d * f * 2 + 8 * tm * d * 4
    return pl.pallas_call(
        functools.partial(_ml_ffn_kernel, alpha=alpha, fc=V7X_MXU_DIM, sub=min(tm, FFN_SUB_ROWS), nh=MLSTM_HEADS),
        grid=(bsz, seq // tm),
        in_specs=[tok(e), tok(e), tok(e, 1), tok(d), rowe, rowe, _resident((e, d)),
                  pl.BlockSpec((1,) + md.shape[1:], lambda b, i: (b, 0, 0)),
                  _resident((d, 2 * f)), _resident((f, d)),
                  pl.BlockSpec(ln.shape, lambda b, i: (0, 0))],
        out_specs=tok(d),
        out_shape=jax.ShapeDtypeStruct((bsz, seq, d), F32),
        compiler_params=_params(nbytes, 2),
        name="mlstm_out_ffn",
    )(hs, xc, xmz, h, norm_g.reshape(1, e), skip.reshape(1, e), w_ml, md, w_in, w_out, ln)


def _pool_conv_layer(h, md, ab, ffn_wi, ffn_wo, ln, alpha, grid2d, tm):
    w_in, pool_w, pool_b, pool_scale, conv_w, conv_b, norm_g, norm_b, w_out = ab
    seq = h.shape[1]
    pool_width = pool_w.shape[0] * POOL_GROUP
    p = _modmm(h, md, w_in, tm)
    band, inv = _pool_constants(seq, grid2d)
    ng = len(POOL_WINDOWS)
    ya = _pool_mixer(p, band, inv, pool_w, pool_b.reshape(ng, 1, POOL_GROUP),
                     pool_scale.reshape(ng, 1, POOL_GROUP), grid2d)
    yb = _conv_mixer(p, pool_width, conv_w, conv_b, norm_g, norm_b, tm)
    return _ab_ffn(ya, yb, h, w_out, md, ffn_wi, ffn_wo, ln, alpha, tm)


def kernel(x, c, ctx, c_ctx, mod_w, mod_b, ln_g, ln_b, ab_w_in, ab_pool_w, ab_pool_b, ab_pool_scale, ab_conv_w, ab_conv_b, ab_norm_g, ab_norm_b, ab_w_out, ml_w_in, ml_conv_w, ml_conv_b, ml_wq, ml_wk, ml_wv, ml_w_gate, ml_b_gate, ml_norm_g, ml_skip, ml_w_out, ffn_w_in, ffn_w_out):
    bsz, seq, d = x.shape
    depth = mod_w.shape[0]
    nh = MLSTM_HEADS
    alpha = (2.0 * depth) ** 0.25
    tm_x, tm_c = 512, ctx.shape[1]
    tm_wide = 2 * tm_x

    n_rows = -(-(bsz + 1) // V7X_SUBLANES) * V7X_SUBLANES
    rows = jnp.concatenate([c, c_ctx[None, :], jnp.zeros((n_rows - bsz - 1, d), F32)], axis=0)
    mods = _mods(rows, mod_w, mod_b)

    h, hc = x, ctx
    for l in range(depth):
        last = l == depth - 1
        even = l % 2 == 0
        j = l // 2
        md_x = mods[l, :bsz].reshape(bsz, 6, d)
        md_c = jnp.broadcast_to(mods[l, bsz].reshape(1, 6, d), (bsz, 6, d))
        ln = jnp.stack([ln_g[l, 0], ln_b[l, 0], ln_g[l, 1], ln_b[l, 1]])
        ffn_wi, ffn_wo = ffn_w_in[l].astype(BF16), ffn_w_out[l].astype(BF16)
        if even:
            ab = (ab_w_in[j].astype(BF16), ab_pool_w[j].astype(BF16), ab_pool_b[j], ab_pool_scale[j],
                  ab_conv_w[j], ab_conv_b[j], ab_norm_g[j], ab_norm_b[j], ab_w_out[j].astype(BF16))
            h = _pool_conv_layer(h, md_x, ab, ffn_wi, ffn_wo, ln, alpha, True, tm_wide)
            if not last:
                hc = _pool_conv_layer(hc, md_c, ab, ffn_wi, ffn_wo, ln, alpha, False, tm_c)
        else:
            e = ml_w_in.shape[2] // 2
            w_in = ml_w_in[j].astype(BF16)
            wq, wv = ml_wq[j].astype(BF16), ml_wv[j].astype(BF16)
            wkt = jnp.swapaxes(ml_wk[j], 1, 2).astype(BF16)
            wg = jnp.transpose(ml_w_gate[j].reshape(2, 3, e, 2, nh), (1, 2, 3, 0, 4)).reshape(3, e, GATE_ROWS)
            wg = jnp.pad(wg, ((0, 0), (0, 0), (0, V7X_LANES - GATE_ROWS)))
            mg = _gatefold(ml_wq[j], ml_wk[j], ml_wv[j], wg)
            bg = jnp.transpose(ml_b_gate[j].reshape(2, 2, nh), (1, 0, 2)).reshape(1, GATE_ROWS)
            bg = jnp.pad(bg, ((0, 0), (0, V7X_LANES - GATE_ROWS)))
            t = SCAN_CHUNK
            idx = jnp.arange(t)
            lower = (idx[None, :] <= idx[:, None])
            tri = jnp.stack([lower, lower.T]).astype(BF16)
            mask = jnp.stack([lower, lower.T]).astype(F32)
            col = jnp.arange(V7X_LANES)
            sel = jnp.where((col >= 2 * nh) & (col < 3 * nh), 1.0,
                            jnp.where((col >= 3 * nh) & (col < 4 * nh), 2.0, 0.0)).reshape(1, V7X_LANES)
            feat = (ml_conv_w[j], ml_conv_b[j], wq, wkt, wv, mg, bg, tri, sel)
            if not last:
                raise NotImplementedError("context output of an mLSTM layer is only needed for depth > 2")
            xm_c = _modmm(hc, md_c, w_in[:, :e], tm_c)
            _, kt_c, v_c, _, _, gt_c = _features(xm_c, *feat, tm_c)
            xmz = _modmm(h, md_x, w_in, tm_wide, silu_from=e)
            q, kt, v, xc, g, gt = _features(xmz, *feat, tm_x)
            hs = _scan(q, kt, v, g, gt, kt_c, v_c, gt_c, mask)
            h = _ml_ffn(hs, xc, xmz, h, ml_norm_g[j], ml_skip[j], ml_w_out[j].astype(BF16), md_x,
                        ffn_wi, ffn_wo, ln, alpha, tm_x)
    return h
```

```python
import functools

import jax
import jax.numpy as jnp
from jax import lax
from jax.experimental import pallas as pl
from jax.experimental.pallas import tpu as pltpu

F32 = jnp.float32
BF16 = jnp.bfloat16

GRID_W = 64
POOL_WINDOWS = (2, 4, 8, 16)
POOL_GROUP = 128
CONV_K = 31
MLSTM_HEADS = 4
MLSTM_CONV_K = 3
LN_EPS = 1e-5

V7X_VMEM_BYTES = 64 * 1024 * 1024
V7X_LANES = 128
V7X_SUBLANES = 8
V7X_BF16_SUBLANES = 16
V7X_MXU_DIM = 256

SCAN_CHUNK = V7X_MXU_DIM
HALO_ROWS = V7X_BF16_SUBLANES
POOL_PAD_ROWS = (max(POOL_WINDOWS) // 2) * GRID_W
GATE_ROWS = 4 * MLSTM_HEADS
FFN_SUB_ROWS = 2 * V7X_MXU_DIM


def _vmem_limit(nbytes):
    return int(min(nbytes * 5 // 4 + (4 << 20), V7X_VMEM_BYTES - (6 << 20)))


def _params(nbytes, n_axes):
    return pltpu.CompilerParams(dimension_semantics=("arbitrary",) * n_axes,
                                vmem_limit_bytes=_vmem_limit(nbytes))


def _resident(shape):
    nd = len(shape)
    return pl.BlockSpec(shape, lambda *_: (0,) * nd, pipeline_mode=pl.Buffered(1))


def _silu(x):
    return x * jax.nn.sigmoid(x)


def _layer_norm(r, g, b):
    mu = jnp.mean(r, axis=-1, keepdims=True)
    xc = r - mu
    var = jnp.mean(xc * xc, axis=-1, keepdims=True)
    return xc * lax.rsqrt(var + LN_EPS) * g + b


def _dot(a, b):
    return jnp.dot(a, b, preferred_element_type=F32)


def _dot_nt(a, b):
    return lax.dot_general(a, b, (((1,), (1,)), ((), ())), preferred_element_type=F32)


def _split_bf16(x):
    hi = x.astype(BF16)
    return hi, (x - hi.astype(F32)).astype(BF16)


def _mods_kernel(x_ref, w_ref, b_ref, o_ref):
    s = _silu(x_ref[...]).astype(BF16)
    o_ref[0] = _dot(s, w_ref[0].astype(BF16)) + b_ref[0]


def _mods(rows, mod_w, mod_b):
    depth, d, n = mod_w.shape
    r = rows.shape[0]
    tn = n // 4
    return pl.pallas_call(
        _mods_kernel,
        grid=(depth, n // tn),
        in_specs=[pl.BlockSpec((r, d), lambda l, j: (0, 0)),
                  pl.BlockSpec((1, d, tn), lambda l, j: (l, 0, j)),
                  pl.BlockSpec((1, 1, tn), lambda l, j: (l, 0, j))],
        out_specs=pl.BlockSpec((1, r, tn), lambda l, j: (l, 0, j)),
        out_shape=jax.ShapeDtypeStruct((depth, r, n), F32),
        compiler_params=_params(2 * d * tn * 4 + d * tn * 2, 2),
        name="mods",
    )(rows, mod_w, mod_b.reshape(depth, 1, n))


def _modmm_kernel(x_ref, md_ref, w_ref, o_ref, *, n_chunk, silu_from):
    u = (x_ref[0] * (1.0 + md_ref[0, 1:2, :]) + md_ref[0, 0:1, :]).astype(BF16)
    for j in range(0, w_ref.shape[1], n_chunk):
        y = _dot(u, w_ref[:, j:j + n_chunk])
        if j >= silu_from:
            y = _silu(y)
        o_ref[0, :, j:j + n_chunk] = y.astype(o_ref.dtype)


def _modmm(x, md, w, tm, silu_from=None):
    bsz, seq, d = x.shape
    n = w.shape[1]
    silu_from = n if silu_from is None else silu_from
    nbytes = 2 * tm * d * 4 + d * n * 2 + 2 * tm * n * 2 + tm * d * 2
    return pl.pallas_call(
        functools.partial(_modmm_kernel, n_chunk=512, silu_from=silu_from),
        grid=(bsz, seq // tm),
        in_specs=[pl.BlockSpec((1, tm, d), lambda b, i: (b, i, 0)),
                  pl.BlockSpec((1,) + md.shape[1:], lambda b, i: (b, 0, 0)), _resident((d, n))],
        out_specs=pl.BlockSpec((1, tm, n), lambda b, i: (b, i, 0)),
        out_shape=jax.ShapeDtypeStruct((bsz, seq, n), BF16),
        compiler_params=_params(nbytes, 2),
        name="modmm",
    )(x, md, w)


def _pool_kernel(a_ref, band_ref, inv_ref, pw_ref, pb_ref, ps_ref, o_ref, pad_ref, *, grid2d):
    g = pl.program_id(1)
    seq = a_ref.shape[1]
    blk = band_ref.shape[1]
    band = band_ref[0]
    for i in range(seq // blk):
        pad_ref[pl.ds(POOL_PAD_ROWS + i * blk, blk), :] = _dot(band, a_ref[0, pl.ds(i * blk, blk), :])
    if grid2d:
        zeros = jnp.zeros((POOL_PAD_ROWS, POOL_GROUP), F32)
        pad_ref[pl.ds(0, POOL_PAD_ROWS), :] = zeros
        pad_ref[pl.ds(POOL_PAD_ROWS + seq, POOL_PAD_ROWS), :] = zeros

    ch = 256

    def finish(r0, summed):
        rows = slice(r0, r0 + ch)
        seg = a_ref[0, rows, :].astype(F32)
        d = (summed * inv_ref[0, rows, :] - seg).astype(BF16)
        y = (_dot(d, pw_ref[0]) + pb_ref[0]) * ps_ref[0]
        o_ref[0, rows, :] = y.astype(o_ref.dtype)

    if not grid2d:
        for r0 in range(0, seq, ch):
            finish(r0, pad_ref[POOL_PAD_ROWS + r0:POOL_PAD_ROWS + r0 + ch, :])
        return

    for gi, w in enumerate(POOL_WINDOWS):
        @pl.when(g == gi)
        def _(w=w):
            for r0 in range(0, seq, ch):
                acc = jnp.zeros((ch, POOL_GROUP), F32)
                for k in range(-(w // 2), w - w // 2):
                    base = POOL_PAD_ROWS + r0 + k * GRID_W
                    acc = acc + pad_ref[base:base + ch, :]
                finish(r0, acc)


def _pool_mixer(p, band, inv, pool_w, pool_b, pool_scale, grid2d):
    bsz, seq, _ = p.shape
    ng = len(POOL_WINDOWS)
    blk = band.shape[1]
    grp = lambda b, g: (g, 0, 0)
    nbytes = 4 * seq * POOL_GROUP * 2 + 2 * seq * V7X_LANES * 4 + (seq + 2 * POOL_PAD_ROWS) * POOL_GROUP * 4
    return pl.pallas_call(
        functools.partial(_pool_kernel, grid2d=grid2d),
        grid=(bsz, ng),
        in_specs=[pl.BlockSpec((1, seq, POOL_GROUP), lambda b, g: (b, 0, g)),
                  pl.BlockSpec((1, blk, blk), grp),
                  pl.BlockSpec((1, seq, 1), grp),
                  pl.BlockSpec((1, POOL_GROUP, POOL_GROUP), grp),
                  pl.BlockSpec((1, 1, POOL_GROUP), grp),
                  pl.BlockSpec((1, 1, POOL_GROUP), grp)],
        out_specs=pl.BlockSpec((1, seq, POOL_GROUP), lambda b, g: (b, 0, g)),
        out_shape=jax.ShapeDtypeStruct((bsz, seq, ng * POOL_GROUP), BF16),
        scratch_shapes=[pltpu.VMEM((seq + 2 * POOL_PAD_ROWS, POOL_GROUP), F32)],
        compiler_params=_params(nbytes, 2),
        name="pool_mixer",
    )(p, band, inv, pool_w, pool_b, pool_scale)


def _pool_constants(seq, grid2d):
    def band1d(n, w):
        t = jnp.arange(n)
        lo = jnp.clip(t - w // 2, 0, n)
        hi = jnp.clip(t + w - w // 2, 0, n)
        s = t[None, :]
        return ((s >= lo[:, None]) & (s < hi[:, None])).astype(F32), (hi - lo).astype(F32)

    bands, invs = [], []
    for w in POOL_WINDOWS:
        if grid2d:
            m, cnt = band1d(GRID_W, w)
            rows = seq // GRID_W
            _, rcnt = band1d(rows, w)
            bands.append(jnp.kron(jnp.eye(V7X_MXU_DIM // GRID_W, dtype=F32), m))
            invs.append(1.0 / (rcnt[:, None] * cnt[None, :]).reshape(seq))
        else:
            m, cnt = band1d(seq, w)
            bands.append(m)
            invs.append(1.0 / cnt)
    return jnp.stack(bands).astype(BF16), jnp.stack(invs).reshape(len(POOL_WINDOWS), seq, 1)


def _conv_kernel(v_ref, g_ref, vp_ref, gp_ref, vn_ref, gn_ref, cw_ref, cb_ref, ng_ref, nb_ref,
                 o_ref, x_ref, y_ref):
    i = pl.program_id(1)
    last = pl.num_programs(1) - 1
    tt, cw = v_ref.shape[1], v_ref.shape[2]

    def glu(v, g):
        return v.astype(F32) * jax.nn.sigmoid(g.astype(F32))

    x_ref[pl.ds(HALO_ROWS, tt), :] = glu(v_ref[0], g_ref[0])
    x_ref[pl.ds(0, HALO_ROWS), :] = jnp.where(i > 0, glu(vp_ref[0], gp_ref[0]), 0.0)
    x_ref[pl.ds(HALO_ROWS + tt, HALO_ROWS), :] = jnp.where(i < last, glu(vn_ref[0], gn_ref[0]), 0.0)

    ct, cl = 128, V7X_LANES
    n_a = (CONV_K + 1) // V7X_SUBLANES
    for c0 in range(0, cw, cl):
        def body(j, carry, c0=c0):
            base = pl.multiple_of(j * ct, ct)
            out = jnp.zeros((ct, cl), F32)
            for r in range(V7X_SUBLANES):
                part = jnp.zeros((ct + V7X_SUBLANES, cl), F32)
                for a in range(n_a):
                    k = V7X_SUBLANES * a + r - 1
                    if 0 <= k < CONV_K:
                        win = x_ref[pl.ds(base + V7X_SUBLANES * a, ct + V7X_SUBLANES), pl.ds(c0, cl)]
                        part = part + cw_ref[k:k + 1, c0:c0 + cl] * win
                out = out + part[r:r + ct, :]
            y_ref[pl.ds(base, ct), pl.ds(c0, cl)] = out
            return carry
        lax.fori_loop(0, tt // ct, body, 0)

    y = y_ref[...] + cb_ref[...]
    o_ref[0] = _silu(_layer_norm(y, ng_ref[...], nb_ref[...])).astype(o_ref.dtype)


def _conv_mixer(p, pool_width, conv_w, conv_b, norm_g, norm_b, tt):
    bsz, seq, _ = p.shape
    k, cw = conv_w.shape
    nb = pool_width // cw
    hb = tt // HALO_ROWS
    nhalo = seq // HALO_ROWS
    cur = lambda j: pl.BlockSpec((1, tt, cw), lambda b, i: (b, i, j))
    prv = lambda j: pl.BlockSpec((1, HALO_ROWS, cw), lambda b, i: (b, jnp.maximum(i * hb - 1, 0), j))
    nxt = lambda j: pl.BlockSpec((1, HALO_ROWS, cw), lambda b, i: (b, jnp.minimum((i + 1) * hb, nhalo - 1), j))
    row = lambda n: pl.BlockSpec((n, cw), lambda b, i: (0, 0))
    nbytes = 4 * tt * cw * 2 + 2 * tt * cw * 2 + (2 * tt + 2 * HALO_ROWS) * cw * 4 + 4 * tt * cw * 4
    return pl.pallas_call(
        _conv_kernel,
        grid=(bsz, seq // tt),
        in_specs=[cur(nb), cur(nb + 1), prv(nb), prv(nb + 1), nxt(nb), nxt(nb + 1),
                  row(k), row(1), row(1), row(1)],
        out_specs=pl.BlockSpec((1, tt, cw), lambda b, i: (b, i, 0)),
        out_shape=jax.ShapeDtypeStruct((bsz, seq, cw), BF16),
        scratch_shapes=[pltpu.VMEM((tt + 2 * HALO_ROWS, cw), F32), pltpu.VMEM((tt, cw), F32)],
        compiler_params=_params(nbytes, 2),
        name="conv_mixer",
    )(p, p, p, p, p, p, conv_w, conv_b.reshape(1, cw), norm_g.reshape(1, cw), norm_b.reshape(1, cw))


def _ffn_rows(h1, md_ref, wi_ref, wo_ref, ln_ref, alpha, fc):
    u = (h1 * (1.0 + md_ref[0, 4:5, :]) + md_ref[0, 3:4, :]).astype(BF16)
    f = wo_ref.shape[0]
    acc = jnp.zeros(h1.shape, F32)
    for c in range(0, f, fc):
        a = _dot(u, wi_ref[:, c:c + fc])
        g = _dot(u, wi_ref[:, f + c:f + c + fc])
        acc = acc + _dot((_silu(a) * g).astype(BF16), wo_ref[c:c + fc, :])
    return _layer_norm(alpha * h1 + md_ref[0, 5:6, :] * acc, ln_ref[2:3, :], ln_ref[3:4, :])


def _ab_ffn_kernel(ya_ref, yb_ref, h_ref, wab_ref, md_ref, wi_ref, wo_ref, ln_ref, o_ref,
                   *, alpha, fc, sub):
    ka = ya_ref.shape[2]
    for r0 in range(0, h_ref.shape[1], sub):
        rs = slice(r0, r0 + sub)
        y = _dot(ya_ref[0, rs, :], wab_ref[0:ka, :]) + _dot(yb_ref[0, rs, :], wab_ref[ka:, :])
        h1 = _layer_norm(alpha * h_ref[0, rs, :] + md_ref[0, 2:3, :] * y, ln_ref[0:1, :], ln_ref[1:2, :])
        o_ref[0, rs, :] = _ffn_rows(h1, md_ref, wi_ref, wo_ref, ln_ref, alpha, fc)


def _ab_ffn(ya, yb, h, w_ab, md, w_in, w_out, ln, alpha, tm):
    bsz, seq, d = h.shape
    ka, kb = ya.shape[2], yb.shape[2]
    f = w_out.shape[0]
    tok = lambda n: pl.BlockSpec((1, tm, n), lambda b, i: (b, i, 0))
    nbytes = (2 * tm * (ka + kb) * 2 + 4 * tm * d * 4 + (ka + kb) * d * 2 + 3 * d * f * 2
              + 6 * tm * d * 4)
    return pl.pallas_call(
        functools.partial(_ab_ffn_kernel, alpha=alpha, fc=V7X_MXU_DIM, sub=min(tm, FFN_SUB_ROWS)),
        grid=(bsz, seq // tm),
        in_specs=[tok(ka), tok(kb), tok(d), _resident((ka + kb, d)),
                  pl.BlockSpec((1,) + md.shape[1:], lambda b, i: (b, 0, 0)),
                  _resident((d, 2 * f)), _resident((f, d)),
                  pl.BlockSpec(ln.shape, lambda b, i: (0, 0))],
        out_specs=tok(d),
        out_shape=jax.ShapeDtypeStruct((bsz, seq, d), F32),
        compiler_params=_params(nbytes, 2),
        name="abproj_ffn",
    )(ya, yb, h, w_ab, md, w_in, w_out, ln)


def _gatefold_kernel(wq_ref, wk_ref, wv_ref, wg_ref, o_ref):
    def mm(a, b):
        ah, al = _split_bf16(a)
        bh, bl = _split_bf16(b)
        return _dot(ah, bh) + _dot(ah, bl) + _dot(al, bh)

    o_ref[0] = (mm(wq_ref[0], wg_ref[0]) + mm(wk_ref[0], wg_ref[1])).astype(o_ref.dtype)
    o_ref[1] = mm(wv_ref[0], wg_ref[2]).astype(o_ref.dtype)


def _gatefold(wq, wk, wv, wg):
    nh, dh, _ = wq.shape
    head = pl.BlockSpec((1, dh, dh), lambda h: (h, 0, 0))
    return pl.pallas_call(
        _gatefold_kernel,
        grid=(nh,),
        in_specs=[head, head, head, pl.BlockSpec((3, dh, V7X_LANES), lambda h: (0, h, 0))],
        out_specs=pl.BlockSpec((2, dh, V7X_LANES), lambda h: (0, h, 0)),
        out_shape=jax.ShapeDtypeStruct((2, nh * dh, V7X_LANES), BF16),
        compiler_params=_params(8 * dh * dh * 4, 1),
        name="gate_fold",
    )(wq, wk, wv, wg)


def _feat_kernel(xm_ref, xp_ref, xn_ref, cw_ref, cb_ref, wq_ref, wkt_ref, wv_ref, mg_ref, bg_ref,
                 tri_ref, sel_ref, q_ref, kt_ref, v_ref, xc_ref, g_ref, gt_ref, *, kscale):
    i = pl.program_id(1)
    last = pl.num_programs(1) - 1
    tm = xm_ref.shape[1]
    nh, dh = wq_ref.shape[0], wq_ref.shape[1]
    row = lax.broadcasted_iota(jnp.int32, (tm, dh), 0)

    gacc = jnp.zeros((tm, V7X_LANES), F32)
    for h in range(nh):
        cs = slice(h * dh, (h + 1) * dh)
        xb = xm_ref[0, :, cs]
        x = xb.astype(F32)
        prev = jnp.where(i > 0, xp_ref[0, HALO_ROWS - 1:HALO_ROWS, cs].astype(F32), 0.0)
        nxt = jnp.where(i < last, xn_ref[0, 0:1, cs].astype(F32), 0.0)
        x_m1 = jnp.where(row == 0, prev, pltpu.roll(x, 1, axis=0))
        x_p1 = jnp.where(row == tm - 1, nxt, pltpu.roll(x, tm - 1, axis=0))
        conv = cw_ref[0:1, cs] * x_m1 + cw_ref[1:2, cs] * x + cw_ref[2:3, cs] * x_p1 + cb_ref[:, cs]
        xc = _silu(conv).astype(BF16)
        xc_ref[0, :, cs] = xc
        q_ref[0, h] = _dot(xc, wq_ref[h]).astype(BF16)
        kt = (_dot_nt(wkt_ref[h], xc) * kscale).astype(BF16)
        for c in range(tm // tri_ref.shape[1]):
            kt_ref[0, h, c] = kt[:, c * tri_ref.shape[1]:(c + 1) * tri_ref.shape[1]]
        v_ref[0, h] = _dot(xb, wv_ref[h]).astype(BF16)
        gacc = gacc + _dot(xc, mg_ref[0, cs, :]) + _dot(xb, mg_ref[1, cs, :])
    g = gacc + bg_ref[...]
    hi, lo = _split_bf16(jax.nn.log_sigmoid(g))
    t = tri_ref.shape[1]
    sel = sel_ref[...]
    for c in range(tm // t):
        rs = slice(c * t, (c + 1) * t)
        fwd = _dot(tri_ref[0], hi[rs]) + _dot(tri_ref[0], lo[rs])
        bwd = _dot(tri_ref[1], hi[rs]) + _dot(tri_ref[1], lo[rs])
        cum = jnp.where(sel == 1.0, fwd, jnp.where(sel == 2.0, bwd, g[rs]))
        g_ref[0, rs, :] = cum
        gt_ref[0, c] = cum.T[0:GATE_ROWS, :]


def _features(xmz, conv_w, conv_b, wq, wkt, wv, mg, bg, tri, sel, tm):
    bsz, seq, _ = xmz.shape
    nh, dh, _ = wq.shape
    e = nh * dh
    hb = tm // HALO_ROWS
    nhalo = seq // HALO_ROWS
    tok = pl.BlockSpec((1, tm, e), lambda b, i: (b, i, 0))
    t = tri.shape[1]
    nbytes = 2 * tm * e * 2 * 5 + 3 * e * dh * 2 + 2 * e * V7X_LANES * 2 + 10 * tm * dh * 4 + 2 * t * t * 2
    return pl.pallas_call(
        functools.partial(_feat_kernel, kscale=float(dh) ** -0.5),
        grid=(bsz, seq // tm),
        in_specs=[tok,
                  pl.BlockSpec((1, HALO_ROWS, e), lambda b, i: (b, jnp.maximum(i * hb - 1, 0), 0)),
                  pl.BlockSpec((1, HALO_ROWS, e), lambda b, i: (b, jnp.minimum((i + 1) * hb, nhalo - 1), 0)),
                  pl.BlockSpec((MLSTM_CONV_K, e), lambda b, i: (0, 0)),
                  pl.BlockSpec((1, e), lambda b, i: (0, 0)),
                  _resident((nh, dh, dh)), _resident((nh, dh, dh)), _resident((nh, dh, dh)),
                  _resident((2, e, V7X_LANES)),
                  pl.BlockSpec((1, V7X_LANES), lambda b, i: (0, 0)),
                  _resident((2, t, t)),
                  pl.BlockSpec((1, V7X_LANES), lambda b, i: (0, 0))],
        out_specs=[pl.BlockSpec((1, nh, tm, dh), lambda b, i: (b, 0, i, 0)),
                   pl.BlockSpec((1, nh, tm // t, dh, t), lambda b, i: (b, 0, i, 0, 0)),
                   pl.BlockSpec((1, nh, tm, dh), lambda b, i: (b, 0, i, 0)),
                   tok,
                   pl.BlockSpec((1, tm, V7X_LANES), lambda b, i: (b, i, 0)),
                   pl.BlockSpec((1, tm // t, GATE_ROWS, t), lambda b, i: (b, i, 0, 0))],
        out_shape=[jax.ShapeDtypeStruct((bsz, nh, seq, dh), BF16),
                   jax.ShapeDtypeStruct((bsz, nh, seq // t, dh, t), BF16),
                   jax.ShapeDtypeStruct((bsz, nh, seq, dh), BF16),
                   jax.ShapeDtypeStruct((bsz, seq, e), BF16),
                   jax.ShapeDtypeStruct((bsz, seq, V7X_LANES), F32),
                   jax.ShapeDtypeStruct((bsz, seq // t, GATE_ROWS, t), F32)],
        compiler_params=_params(nbytes, 2),
        name="mlstm_features",
    )(xmz, xmz, xmz, conv_w, conv_b.reshape(1, e), wq, wkt, wv, mg, bg, tri, sel)


def _scan_kernel(qf_ref, ktf_ref, vf_ref, gf_ref, gtf_ref, qb_ref, ktb_ref, vb_ref, gb_ref, gtb_ref,
                 ktx_ref, vx_ref, gtx_ref, mask_ref, o_ref, c_ref, n_ref, m_ref, *, nc, nh):
    hg = qf_ref.shape[1]
    h0 = pl.program_id(1) * hg
    c = pl.program_id(2)
    t, dh = qf_ref.shape[2], qf_ref.shape[3]
    lane_g = lax.broadcasted_iota(jnp.int32, (1, V7X_LANES), 1)
    sub = lax.broadcasted_iota(jnp.int32, (V7X_BF16_SUBLANES, t), 0)

    def gate_rows(gt_ref, dd, h):
        ig_r = gt_ref[0, 0, pl.ds(dd * nh + h, 1), :]
        b_r = gt_ref[0, 0, pl.ds(2 * nh + dd * nh + h, 1), :]
        last = (t - 1) * (1 - dd)
        return ig_r, b_r, b_r[:, last:last + 1]

    def update_state(st, kt, vb, ig_r, b_r, b_last):
        m = m_ref[st]
        w_r = b_last - b_r + ig_r
        m_new = jnp.maximum(b_last + m, jnp.max(w_r, axis=1, keepdims=True))
        decay = jnp.exp(b_last + m - m_new)
        wexp = jnp.exp(w_r - m_new)
        kw = kt * wexp.astype(BF16)
        c_ref[st] = decay * c_ref[st] + _dot(kw, vb)
        w_hi = wexp.astype(BF16).astype(F32)
        w2 = jnp.where(sub == 0, w_hi, jnp.where(sub == 1, wexp - w_hi, 0.0)).astype(BF16)
        n_ref[st] = decay * n_ref[st] + jnp.sum(_dot_nt(w2, kt), axis=0, keepdims=True)
        m_ref[st] = m_new

    @pl.when(c == 0)
    def _():
        c_ref[...] = jnp.zeros(c_ref.shape, F32)
        n_ref[...] = jnp.zeros(n_ref.shape, F32)
        m_ref[...] = jnp.zeros(m_ref.shape, F32)
        o_ref[...] = jnp.zeros(o_ref.shape, o_ref.dtype)
        for hh in range(hg):
            for dd in range(2):
                update_state(2 * hh + dd, ktx_ref[0, hh, 0], vx_ref[0, hh], *gate_rows(gtx_ref, dd, h0 + hh))

    streams = ((qf_ref, ktf_ref, vf_ref, gf_ref, gtf_ref), (qb_ref, ktb_ref, vb_ref, gb_ref, gtb_ref))
    rows = (pl.ds(pl.multiple_of(c * t, t), t), pl.ds(pl.multiple_of((nc - 1 - c) * t, t), t))
    outs, pending = [], []
    for hh in range(hg):
        for dd, (q_ref, kt_ref, v_ref, g_ref, gt_ref) in enumerate(streams):
            st = 2 * hh + dd
            qb, kt, vb = q_ref[0, hh], kt_ref[0, hh, 0], v_ref[0, hh]
            ig_r, b_r, b_last = gate_rows(gt_ref, dd, h0 + hh)
            b_c = jnp.sum(jnp.where(lane_g == 2 * nh + dd * nh + h0 + hh, g_ref[0], 0.0),
                          axis=1, keepdims=True)
            m = m_ref[st]
            dmat = jnp.where(mask_ref[dd] > 0.0, b_c - b_r + ig_r, -jnp.inf)
            inter = b_c + m
            m_t = jnp.maximum(inter, jnp.max(dmat, axis=1, keepdims=True))
            s = _dot(qb, kt) * jnp.exp(dmat - m_t)
            iw = jnp.exp(inter - m_t)
            num = iw * _dot(qb, c_ref[st].astype(BF16)) + _dot(s.astype(BF16), vb)
            den = (iw * jnp.sum(qb.astype(F32) * n_ref[st], axis=1, keepdims=True)
                   + jnp.sum(s, axis=1, keepdims=True))
            outs.append((hh, dd, num * (1.0 / jnp.maximum(jnp.abs(den), jnp.exp(-m_t)))))
            pending.append((st, kt, vb, ig_r, b_r, b_last))

    for hh, dd, hout in outs:
        cs = slice(hh * dh, (hh + 1) * dh)
        o_ref[0, rows[dd], cs] = (o_ref[0, rows[dd], cs].astype(F32) + hout).astype(o_ref.dtype)

    for args in pending:
        update_state(*args)


def _scan(q, kt, v, g, gt, ktx, vx, gtx, mask, hg):
    bsz, nh, seq, dh = q.shape
    t = SCAN_CHUNK
    nc = seq // t
    assert nc % 2 == 0, "both directions advance together: needs an even number of chunks"
    assert vx.shape[2] == t, "context length must equal one scan chunk"
    fwd = lambda c: c
    bwd = lambda c: nc - 1 - c
    zero = lambda c: 0
    tok = lambda f: pl.BlockSpec((1, hg, t, dh), lambda b, h, c: (b, h, f(c), 0))
    ktb = lambda f: pl.BlockSpec((1, hg, 1, dh, t), lambda b, h, c: (b, h, f(c), 0, 0))
    gcol = lambda f: pl.BlockSpec((1, t, V7X_LANES), lambda b, h, c: (b, f(c), 0))
    grow = lambda f: pl.BlockSpec((1, 1, GATE_ROWS, t), lambda b, h, c: (b, f(c), 0, 0))
    nbytes = hg * (2 * seq * dh * 2 + 16 * t * dh * 2 + 4 * dh * dh * 4 + 24 * t * t * 4
                   + 8 * t * dh * 4) + 4 * t * V7X_LANES * 4 + 2 * t * t * 4
    return pl.pallas_call(
        functools.partial(_scan_kernel, nc=nc, nh=nh),
        grid=(bsz, nh // hg, nc),
        in_specs=[tok(fwd), ktb(fwd), tok(fwd), gcol(fwd), grow(fwd),
                  tok(bwd), ktb(bwd), tok(bwd), gcol(bwd), grow(bwd),
                  ktb(zero), tok(zero), grow(zero), _resident((2, t, t))],
        out_specs=pl.BlockSpec((1, seq, hg * dh), lambda b, h, c: (b, 0, h)),
        out_shape=jax.ShapeDtypeStruct((bsz, seq, nh * dh), BF16),
        scratch_shapes=[pltpu.VMEM((2 * hg, dh, dh), F32), pltpu.VMEM((2 * hg, 1, dh), F32),
                        pltpu.VMEM((2 * hg, 1, 1), F32)],
        compiler_params=_params(nbytes, 3),
        name="mlstm_scan",
    )(q, kt, v, g, gt, q, kt, v, g, gt, ktx, vx, gtx, mask)


def _ml_ffn_kernel(hs_ref, xc_ref, sz_ref, h_ref, ng_ref, sk_ref, wml_ref, md_ref, wi_ref, wo_ref,
                   ln_ref, o_ref, *, alpha, fc, sub, nh):
    e = hs_ref.shape[2]
    dh = e // nh
    for r0 in range(0, h_ref.shape[1], sub):
        rs = slice(r0, r0 + sub)
        acc = jnp.zeros((sub, h_ref.shape[2]), F32)
        for h in range(nh):
            cs = slice(h * dh, (h + 1) * dh)
            hh = hs_ref[0, rs, cs].astype(F32)
            mu = jnp.mean(hh, axis=-1, keepdims=True)
            hc = hh - mu
            var = jnp.mean(hc * hc, axis=-1, keepdims=True)
            hn = (hc * (lax.rsqrt(var + LN_EPS) * ng_ref[:, cs])).astype(BF16)
            y = (hn + sk_ref[:, cs].astype(BF16) * xc_ref[0, rs, cs]) * sz_ref[0, rs, cs]
            acc = acc + _dot(y, wml_ref[cs, :])
        h1 = _layer_norm(alpha * h_ref[0, rs, :] + md_ref[0, 2:3, :] * acc, ln_ref[0:1, :], ln_ref[1:2, :])
        o_ref[0, rs, :] = _ffn_rows(h1, md_ref, wi_ref, wo_ref, ln_ref, alpha, fc)


def _ml_ffn(hs, xc, xmz, h, norm_g, skip, w_ml, md, w_in, w_out, ln, alpha, tm):
    bsz, seq, d = h.shape
    e = hs.shape[2]
    f = w_out.shape[0]
    tok = lambda n, j=0: pl.BlockSpec((1, tm, n), lambda b, i: (b, i, j))
    rowe = pl.BlockSpec((1, e), lambda b, i: (0, 0))
    nbytes = 2 * tm * e * 6 + 4 * tm * d * 4 + e * d * 2 + 3 * d * f * 2 + 8 * tm * d * 4
    return pl.pallas_call(
        functools.partial(_ml_ffn_kernel, alpha=alpha, fc=V7X_MXU_DIM, sub=min(tm, FFN_SUB_ROWS), nh=MLSTM_HEADS),
        grid=(bsz, seq // tm),
        in_specs=[tok(e), tok(e), tok(e, 1), tok(d), rowe, rowe, _resident((e, d)),
                  pl.BlockSpec((1,) + md.shape[1:], lambda b, i: (b, 0, 0)),
                  _resident((d, 2 * f)), _resident((f, d)),
                  pl.BlockSpec(ln.shape, lambda b, i: (0, 0))],
        out_specs=tok(d),
        out_shape=jax.ShapeDtypeStruct((bsz, seq, d), F32),
        compiler_params=_params(nbytes, 2),
        name="mlstm_out_ffn",
    )(hs, xc, xmz, h, norm_g.reshape(1, e), skip.reshape(1, e), w_ml, md, w_in, w_out, ln)


def _pool_conv_layer(h, md, ab, ffn_wi, ffn_wo, ln, alpha, grid2d, tm):
    w_in, pool_w, pool_b, pool_scale, conv_w, conv_b, norm_g, norm_b, w_out = ab
    seq = h.shape[1]
    pool_width = pool_w.shape[0] * POOL_GROUP
    p = _modmm(h, md, w_in, tm)
    band, inv = _pool_constants(seq, grid2d)
    ng = len(POOL_WINDOWS)
    ya = _pool_mixer(p, band, inv, pool_w, pool_b.reshape(ng, 1, POOL_GROUP),
                     pool_scale.reshape(ng, 1, POOL_GROUP), grid2d)
    yb = _conv_mixer(p, pool_width, conv_w, conv_b, norm_g, norm_b, tm)
    return _ab_ffn(ya, yb, h, w_out, md, ffn_wi, ffn_wo, ln, alpha, tm)


def kernel(x, c, ctx, c_ctx, mod_w, mod_b, ln_g, ln_b, ab_w_in, ab_pool_w, ab_pool_b, ab_pool_scale, ab_conv_w, ab_conv_b, ab_norm_g, ab_norm_b, ab_w_out, ml_w_in, ml_conv_w, ml_conv_b, ml_wq, ml_wk, ml_wv, ml_w_gate, ml_b_gate, ml_norm_g, ml_skip, ml_w_out, ffn_w_in, ffn_w_out):
    bsz, seq, d = x.shape
    depth = mod_w.shape[0]
    nh = MLSTM_HEADS
    alpha = (2.0 * depth) ** 0.25
    tm_x, tm_c = 512, ctx.shape[1]
    tm_wide = 2 * tm_x

    n_rows = -(-(bsz + 1) // V7X_SUBLANES) * V7X_SUBLANES
    rows = jnp.concatenate([c, c_ctx[None, :], jnp.zeros((n_rows - bsz - 1, d), F32)], axis=0)
    mods = _mods(rows, mod_w, mod_b)

    h, hc = x, ctx
    for l in range(depth):
        last = l == depth - 1
        even = l % 2 == 0
        j = l // 2
        md_x = mods[l, :bsz].reshape(bsz, 6, d)
        md_c = jnp.broadcast_to(mods[l, bsz].reshape(1, 6, d), (bsz, 6, d))
        ln = jnp.stack([ln_g[l, 0], ln_b[l, 0], ln_g[l, 1], ln_b[l, 1]])
        ffn_wi, ffn_wo = ffn_w_in[l].astype(BF16), ffn_w_out[l].astype(BF16)
        if even:
            ab = (ab_w_in[j].astype(BF16), ab_pool_w[j].astype(BF16), ab_pool_b[j], ab_pool_scale[j],
                  ab_conv_w[j], ab_conv_b[j], ab_norm_g[j], ab_norm_b[j], ab_w_out[j].astype(BF16))
            h = _pool_conv_layer(h, md_x, ab, ffn_wi, ffn_wo, ln, alpha, True, tm_wide)
            if not last:
                hc = _pool_conv_layer(hc, md_c, ab, ffn_wi, ffn_wo, ln, alpha, False, tm_c)
        else:
            e = ml_w_in.shape[2] // 2
            w_in = ml_w_in[j].astype(BF16)
            wq, wv = ml_wq[j].astype(BF16), ml_wv[j].astype(BF16)
            wkt = jnp.swapaxes(ml_wk[j], 1, 2).astype(BF16)
            wg = jnp.transpose(ml_w_gate[j].reshape(2, 3, e, 2, nh), (1, 2, 3, 0, 4)).reshape(3, e, GATE_ROWS)
            wg = jnp.pad(wg, ((0, 0), (0, 0), (0, V7X_LANES - GATE_ROWS)))
            mg = _gatefold(ml_wq[j], ml_wk[j], ml_wv[j], wg)
            bg = jnp.transpose(ml_b_gate[j].reshape(2, 2, nh), (1, 0, 2)).reshape(1, GATE_ROWS)
            bg = jnp.pad(bg, ((0, 0), (0, V7X_LANES - GATE_ROWS)))
            t = SCAN_CHUNK
            idx = jnp.arange(t)
            lower = (idx[None, :] <= idx[:, None])
            tri = jnp.stack([lower, lower.T]).astype(BF16)
            mask = jnp.stack([lower, lower.T]).astype(F32)
            col = jnp.arange(V7X_LANES)
            sel = jnp.where((col >= 2 * nh) & (col < 3 * nh), 1.0,
                            jnp.where((col >= 3 * nh) & (col < 4 * nh), 2.0, 0.0)).reshape(1, V7X_LANES)
            feat = (ml_conv_w[j], ml_conv_b[j], wq, wkt, wv, mg, bg, tri, sel)
            if not last:
                raise NotImplementedError("context output of an mLSTM layer is only needed for depth > 2")
            xm_c = _modmm(hc, md_c, w_in[:, :e], tm_c)
            _, kt_c, v_c, _, _, gt_c = _features(xm_c, *feat, tm_c)
            xmz = _modmm(h, md_x, w_in, tm_wide, silu_from=e)
            q, kt, v, xc, g, gt = _features(xmz, *feat, tm_x)
            hs = _scan(q, kt, v, g, gt, kt_c, v_c, gt_c, mask, hg=2)
            h = _ml_ffn(hs, xc, xmz, h, ml_norm_g[j], ml_skip[j], ml_w_out[j].astype(BF16), md_x,
                        ffn_wi, ffn_wo, ln, alpha, tm_x)
    return h
```

```python
import functools

import jax
import jax.numpy as jnp
from jax import lax
from jax.experimental import pallas as pl
from jax.experimental.pallas import tpu as pltpu

F32 = jnp.float32
BF16 = jnp.bfloat16

GRID_W = 64
POOL_WINDOWS = (2, 4, 8, 16)
POOL_GROUP = 128
CONV_K = 31
MLSTM_HEADS = 4
MLSTM_CONV_K = 3
LN_EPS = 1e-5

V7X_VMEM_BYTES = 64 * 1024 * 1024
V7X_LANES = 128
V7X_SUBLANES = 8
V7X_BF16_SUBLANES = 16
V7X_MXU_DIM = 256

SCAN_CHUNK = V7X_MXU_DIM
HALO_ROWS = V7X_BF16_SUBLANES
POOL_PAD_ROWS = (max(POOL_WINDOWS) // 2) * GRID_W
GATE_ROWS = 4 * MLSTM_HEADS
FFN_SUB_ROWS = 2 * V7X_MXU_DIM


def _vmem_limit(nbytes):
    return int(min(nbytes * 5 // 4 + (4 << 20), V7X_VMEM_BYTES - (6 << 20)))


def _params(nbytes, n_axes):
    return pltpu.CompilerParams(dimension_semantics=("arbitrary",) * n_axes,
                                vmem_limit_bytes=_vmem_limit(nbytes))


def _resident(shape):
    nd = len(shape)
    return pl.BlockSpec(shape, lambda *_: (0,) * nd, pipeline_mode=pl.Buffered(1))


def _silu(x):
    return x * jax.nn.sigmoid(x)


def _layer_norm(r, g, b):
    mu = jnp.mean(r, axis=-1, keepdims=True)
    xc = r - mu
    var = jnp.mean(xc * xc, axis=-1, keepdims=True)
    return xc * lax.rsqrt(var + LN_EPS) * g + b


def _dot(a, b):
    return jnp.dot(a, b, preferred_element_type=F32)


def _dot_nt(a, b):
    return lax.dot_general(a, b, (((1,), (1,)), ((), ())), preferred_element_type=F32)


def _split_bf16(x):
    hi = x.astype(BF16)
    return hi, (x - hi.astype(F32)).astype(BF16)


def _mods_kernel(x_ref, w_ref, b_ref, o_ref):
    s = _silu(x_ref[...]).astype(BF16)
    o_ref[0] = _dot(s, w_ref[0].astype(BF16)) + b_ref[0]


def _mods(rows, mod_w, mod_b):
    depth, d, n = mod_w.shape
    r = rows.shape[0]
    tn = n // 4
    return pl.pallas_call(
        _mods_kernel,
        grid=(depth, n // tn),
        in_specs=[pl.BlockSpec((r, d), lambda l, j: (0, 0)),
                  pl.BlockSpec((1, d, tn), lambda l, j: (l, 0, j)),
                  pl.BlockSpec((1, 1, tn), lambda l, j: (l, 0, j))],
        out_specs=pl.BlockSpec((1, r, tn), lambda l, j: (l, 0, j)),
        out_shape=jax.ShapeDtypeStruct((depth, r, n), F32),
        compiler_params=_params(2 * d * tn * 4 + d * tn * 2, 2),
        name="mods",
    )(rows, mod_w, mod_b.reshape(depth, 1, n))


def _modmm_kernel(x_ref, md_ref, w_ref, o_ref, *, n_chunk, silu_from):
    u = (x_ref[0] * (1.0 + md_ref[0, 1:2, :]) + md_ref[0, 0:1, :]).astype(BF16)
    for j in range(0, w_ref.shape[1], n_chunk):
        y = _dot(u, w_ref[:, j:j + n_chunk])
        if j >= silu_from:
            y = _silu(y)
        o_ref[0, :, j:j + n_chunk] = y.astype(o_ref.dtype)


def _modmm(x, md, w, tm, silu_from=None):
    bsz, seq, d = x.shape
    n = w.shape[1]
    silu_from = n if silu_from is None else silu_from
    nbytes = 2 * tm * d * 4 + d * n * 2 + 2 * tm * n * 2 + tm * d * 2
    return pl.pallas_call(
        functools.partial(_modmm_kernel, n_chunk=512, silu_from=silu_from),
        grid=(bsz, seq // tm),
        in_specs=[pl.BlockSpec((1, tm, d), lambda b, i: (b, i, 0)),
                  pl.BlockSpec((1,) + md.shape[1:], lambda b, i: (b, 0, 0)), _resident((d, n))],
        out_specs=pl.BlockSpec((1, tm, n), lambda b, i: (b, i, 0)),
        out_shape=jax.ShapeDtypeStruct((bsz, seq, n), BF16),
        compiler_params=_params(nbytes, 2),
        name="modmm",
    )(x, md, w)


def _pool_kernel(a_ref, band_ref, inv_ref, pw_ref, pb_ref, ps_ref, o_ref, pad_ref, *, grid2d):
    g = pl.program_id(1)
    seq = a_ref.shape[1]
    blk = band_ref.shape[1]
    band = band_ref[0]
    for i in range(seq // blk):
        pad_ref[pl.ds(POOL_PAD_ROWS + i * blk, blk), :] = _dot(band, a_ref[0, pl.ds(i * blk, blk), :])
    if grid2d:
        zeros = jnp.zeros((POOL_PAD_ROWS, POOL_GROUP), F32)
        pad_ref[pl.ds(0, POOL_PAD_ROWS), :] = zeros
        pad_ref[pl.ds(POOL_PAD_ROWS + seq, POOL_PAD_ROWS), :] = zeros

    ch = 256

    def finish(r0, summed):
        rows = slice(r0, r0 + ch)
        seg = a_ref[0, rows, :].astype(F32)
        d = (summed * inv_ref[0, rows, :] - seg).astype(BF16)
        y = (_dot(d, pw_ref[0]) + pb_ref[0]) * ps_ref[0]
        o_ref[0, rows, :] = y.astype(o_ref.dtype)

    if not grid2d:
        for r0 in range(0, seq, ch):
            finish(r0, pad_ref[POOL_PAD_ROWS + r0:POOL_PAD_ROWS + r0 + ch, :])
        return

    for gi, w in enumerate(POOL_WINDOWS):
        @pl.when(g == gi)
        def _(w=w):
            for r0 in range(0, seq, ch):
                acc = jnp.zeros((ch, POOL_GROUP), F32)
                for k in range(-(w // 2), w - w // 2):
                    base = POOL_PAD_ROWS + r0 + k * GRID_W
                    acc = acc + pad_ref[base:base + ch, :]
                finish(r0, acc)


def _pool_mixer(p, band, inv, pool_w, pool_b, pool_scale, grid2d):
    bsz, seq, _ = p.shape
    ng = len(POOL_WINDOWS)
    blk = band.shape[1]
    grp = lambda b, g: (g, 0, 0)
    nbytes = 4 * seq * POOL_GROUP * 2 + 2 * seq * V7X_LANES * 4 + (seq + 2 * POOL_PAD_ROWS) * POOL_GROUP * 4
    return pl.pallas_call(
        functools.partial(_pool_kernel, grid2d=grid2d),
        grid=(bsz, ng),
        in_specs=[pl.BlockSpec((1, seq, POOL_GROUP), lambda b, g: (b, 0, g)),
                  pl.BlockSpec((1, blk, blk), grp),
                  pl.BlockSpec((1, seq, 1), grp),
                  pl.BlockSpec((1, POOL_GROUP, POOL_GROUP), grp),
                  pl.BlockSpec((1, 1, POOL_GROUP), grp),
                  pl.BlockSpec((1, 1, POOL_GROUP), grp)],
        out_specs=pl.BlockSpec((1, seq, POOL_GROUP), lambda b, g: (b, 0, g)),
        out_shape=jax.ShapeDtypeStruct((bsz, seq, ng * POOL_GROUP), BF16),
        scratch_shapes=[pltpu.VMEM((seq + 2 * POOL_PAD_ROWS, POOL_GROUP), F32)],
        compiler_params=_params(nbytes, 2),
        name="pool_mixer",
    )(p, band, inv, pool_w, pool_b, pool_scale)


def _pool_constants(seq, grid2d):
    def band1d(n, w):
        t = jnp.arange(n)
        lo = jnp.clip(t - w // 2, 0, n)
        hi = jnp.clip(t + w - w // 2, 0, n)
        s = t[None, :]
        return ((s >= lo[:, None]) & (s < hi[:, None])).astype(F32), (hi - lo).astype(F32)

    bands, invs = [], []
    for w in POOL_WINDOWS:
        if grid2d:
            m, cnt = band1d(GRID_W, w)
            rows = seq // GRID_W
            _, rcnt = band1d(rows, w)
            bands.append(jnp.kron(jnp.eye(V7X_MXU_DIM // GRID_W, dtype=F32), m))
            invs.append(1.0 / (rcnt[:, None] * cnt[None, :]).reshape(seq))
        else:
            m, cnt = band1d(seq, w)
            bands.append(m)
            invs.append(1.0 / cnt)
    return jnp.stack(bands).astype(BF16), jnp.stack(invs).reshape(len(POOL_WINDOWS), seq, 1)


def _conv_kernel(v_ref, g_ref, vp_ref, gp_ref, vn_ref, gn_ref, cw_ref, cb_ref, ng_ref, nb_ref,
                 o_ref, x_ref, y_ref):
    i = pl.program_id(1)
    last = pl.num_programs(1) - 1
    tt, cw = v_ref.shape[1], v_ref.shape[2]

    def glu(v, g):
        return v.astype(F32) * jax.nn.sigmoid(g.astype(F32))

    x_ref[pl.ds(HALO_ROWS, tt), :] = glu(v_ref[0], g_ref[0])
    x_ref[pl.ds(0, HALO_ROWS), :] = jnp.where(i > 0, glu(vp_ref[0], gp_ref[0]), 0.0)
    x_ref[pl.ds(HALO_ROWS + tt, HALO_ROWS), :] = jnp.where(i < last, glu(vn_ref[0], gn_ref[0]), 0.0)

    ct, cl = 128, V7X_LANES
    n_a = (CONV_K + 1) // V7X_SUBLANES
    for c0 in range(0, cw, cl):
        def body(j, carry, c0=c0):
            base = pl.multiple_of(j * ct, ct)
            out = jnp.zeros((ct, cl), F32)
            for r in range(V7X_SUBLANES):
                part = jnp.zeros((ct + V7X_SUBLANES, cl), F32)
                for a in range(n_a):
                    k = V7X_SUBLANES * a + r - 1
                    if 0 <= k < CONV_K:
                        win = x_ref[pl.ds(base + V7X_SUBLANES * a, ct + V7X_SUBLANES), pl.ds(c0, cl)]
                        part = part + cw_ref[k:k + 1, c0:c0 + cl] * win
                out = out + part[r:r + ct, :]
            y_ref[pl.ds(base, ct), pl.ds(c0, cl)] = out
            return carry
        lax.fori_loop(0, tt // ct, body, 0)

    y = y_ref[...] + cb_ref[...]
    o_ref[0] = _silu(_layer_norm(y, ng_ref[...], nb_ref[...])).astype(o_ref.dtype)


def _conv_mixer(p, pool_width, conv_w, conv_b, norm_g, norm_b, tt):
    bsz, seq, _ = p.shape
    k, cw = conv_w.shape
    nb = pool_width // cw
    hb = tt // HALO_ROWS
    nhalo = seq // HALO_ROWS
    cur = lambda j: pl.BlockSpec((1, tt, cw), lambda b, i: (b, i, j))
    prv = lambda j: pl.BlockSpec((1, HALO_ROWS, cw), lambda b, i: (b, jnp.maximum(i * hb - 1, 0), j))
    nxt = lambda j: pl.BlockSpec((1, HALO_ROWS, cw), lambda b, i: (b, jnp.minimum((i + 1) * hb, nhalo - 1), j))
    row = lambda n: pl.BlockSpec((n, cw), lambda b, i: (0, 0))
    nbytes = 4 * tt * cw * 2 + 2 * tt * cw * 2 + (2 * tt + 2 * HALO_ROWS) * cw * 4 + 4 * tt * cw * 4
    return pl.pallas_call(
        _conv_kernel,
        grid=(bsz, seq // tt),
        in_specs=[cur(nb), cur(nb + 1), prv(nb), prv(nb + 1), nxt(nb), nxt(nb + 1),
                  row(k), row(1), row(1), row(1)],
        out_specs=pl.BlockSpec((1, tt, cw), lambda b, i: (b, i, 0)),
        out_shape=jax.ShapeDtypeStruct((bsz, seq, cw), BF16),
        scratch_shapes=[pltpu.VMEM((tt + 2 * HALO_ROWS, cw), F32), pltpu.VMEM((tt, cw), F32)],
        compiler_params=_params(nbytes, 2),
        name="conv_mixer",
    )(p, p, p, p, p, p, conv_w, conv_b.reshape(1, cw), norm_g.reshape(1, cw), norm_b.reshape(1, cw))


def _ffn_subtiles(prologue, o_ref, sub, md_ref, wi_ref, wo_ref, ln_ref, alpha, fc):
    f = wo_ref.shape[0]
    tiles = [slice(r0, r0 + sub) for r0 in range(0, o_ref.shape[1], sub)]
    h1 = prologue(tiles[0])
    for s, rs in enumerate(tiles):
        u = (h1 * (1.0 + md_ref[0, 4:5, :]) + md_ref[0, 3:4, :]).astype(BF16)
        acc = jnp.zeros(h1.shape, F32)
        h1_next = None
        for ci, c in enumerate(range(0, f, fc)):
            a = _dot(u, wi_ref[:, c:c + fc])
            g = _dot(u, wi_ref[:, f + c:f + c + fc])
            acc = acc + _dot((_silu(a) * g).astype(BF16), wo_ref[c:c + fc, :])
            if ci == 1 and s + 1 < len(tiles):
                h1_next = prologue(tiles[s + 1])
        o_ref[0, rs, :] = _layer_norm(alpha * h1 + md_ref[0, 5:6, :] * acc, ln_ref[2:3, :], ln_ref[3:4, :])
        h1 = h1_next


def _ab_ffn_kernel(ya_ref, yb_ref, h_ref, wab_ref, md_ref, wi_ref, wo_ref, ln_ref, o_ref,
                   *, alpha, fc, sub):
    ka = ya_ref.shape[2]

    def prologue(rs):
        y = _dot(ya_ref[0, rs, :], wab_ref[0:ka, :]) + _dot(yb_ref[0, rs, :], wab_ref[ka:, :])
        return _layer_norm(alpha * h_ref[0, rs, :] + md_ref[0, 2:3, :] * y, ln_ref[0:1, :], ln_ref[1:2, :])

    _ffn_subtiles(prologue, o_ref, sub, md_ref, wi_ref, wo_ref, ln_ref, alpha, fc)


def _ab_ffn(ya, yb, h, w_ab, md, w_in, w_out, ln, alpha, tm):
    bsz, seq, d = h.shape
    ka, kb = ya.shape[2], yb.shape[2]
    f = w_out.shape[0]
    tok = lambda n: pl.BlockSpec((1, tm, n), lambda b, i: (b, i, 0))
    nbytes = (2 * tm * (ka + kb) * 2 + 4 * tm * d * 4 + (ka + kb) * d * 2 + 3 * d * f * 2
              + 6 * tm * d * 4)
    return pl.pallas_call(
        functools.partial(_ab_ffn_kernel, alpha=alpha, fc=V7X_MXU_DIM, sub=min(tm, FFN_SUB_ROWS)),
        grid=(bsz, seq // tm),
        in_specs=[tok(ka), tok(kb), tok(d), _resident((ka + kb, d)),
                  pl.BlockSpec((1,) + md.shape[1:], lambda b, i: (b, 0, 0)),
                  _resident((d, 2 * f)), _resident((f, d)),
                  pl.BlockSpec(ln.shape, lambda b, i: (0, 0))],
        out_specs=tok(d),
        out_shape=jax.ShapeDtypeStruct((bsz, seq, d), F32),
        compiler_params=_params(nbytes, 2),
        name="abproj_ffn",
    )(ya, yb, h, w_ab, md, w_in, w_out, ln)


def _gatefold_kernel(wq_ref, wk_ref, wv_ref, wg_ref, o_ref):
    def mm(a, b):
        ah, al = _split_bf16(a)
        bh, bl = _split_bf16(b)
        return _dot(ah, bh) + _dot(ah, bl) + _dot(al, bh)

    o_ref[0] = (mm(wq_ref[0], wg_ref[0]) + mm(wk_ref[0], wg_ref[1])).astype(o_ref.dtype)
    o_ref[1] = mm(wv_ref[0], wg_ref[2]).astype(o_ref.dtype)


def _gatefold(wq, wk, wv, wg):
    nh, dh, _ = wq.shape
    head = pl.BlockSpec((1, dh, dh), lambda h: (h, 0, 0))
    return pl.pallas_call(
        _gatefold_kernel,
        grid=(nh,),
        in_specs=[head, head, head, pl.BlockSpec((3, dh, V7X_LANES), lambda h: (0, h, 0))],
        out_specs=pl.BlockSpec((2, dh, V7X_LANES), lambda h: (0, h, 0)),
        out_shape=jax.ShapeDtypeStruct((2, nh * dh, V7X_LANES), BF16),
        compiler_params=_params(8 * dh * dh * 4, 1),
        name="gate_fold",
    )(wq, wk, wv, wg)


def _feat_kernel(xm_ref, xp_ref, xn_ref, cw_ref, cb_ref, wq_ref, wkt_ref, wv_ref, mg_ref, bg_ref,
                 tri_ref, sel_ref, q_ref, kt_ref, v_ref, xc_ref, g_ref, gt_ref, *, kscale):
    i = pl.program_id(1)
    last = pl.num_programs(1) - 1
    tm = xm_ref.shape[1]
    nh, dh = wq_ref.shape[0], wq_ref.shape[1]
    row = lax.broadcasted_iota(jnp.int32, (tm, dh), 0)

    gacc = jnp.zeros((tm, V7X_LANES), F32)
    for h in range(nh):
        cs = slice(h * dh, (h + 1) * dh)
        xb = xm_ref[0, :, cs]
        x = xb.astype(F32)
        prev = jnp.where(i > 0, xp_ref[0, HALO_ROWS - 1:HALO_ROWS, cs].astype(F32), 0.0)
        nxt = jnp.where(i < last, xn_ref[0, 0:1, cs].astype(F32), 0.0)
        x_m1 = jnp.where(row == 0, prev, pltpu.roll(x, 1, axis=0))
        x_p1 = jnp.where(row == tm - 1, nxt, pltpu.roll(x, tm - 1, axis=0))
        conv = cw_ref[0:1, cs] * x_m1 + cw_ref[1:2, cs] * x + cw_ref[2:3, cs] * x_p1 + cb_ref[:, cs]
        xc = _silu(conv).astype(BF16)
        xc_ref[0, :, cs] = xc
        q_ref[0, h] = _dot(xc, wq_ref[h]).astype(BF16)
        kt = (_dot_nt(wkt_ref[h], xc) * kscale).astype(BF16)
        for c in range(tm // tri_ref.shape[1]):
            kt_ref[0, h, c] = kt[:, c * tri_ref.shape[1]:(c + 1) * tri_ref.shape[1]]
        v_ref[0, h] = _dot(xb, wv_ref[h]).astype(BF16)
        gacc = gacc + _dot(xc, mg_ref[0, cs, :]) + _dot(xb, mg_ref[1, cs, :])
    g = gacc + bg_ref[...]
    hi, lo = _split_bf16(jax.nn.log_sigmoid(g))
    t = tri_ref.shape[1]
    sel = sel_ref[...]
    for c in range(tm // t):
        rs = slice(c * t, (c + 1) * t)
        fwd = _dot(tri_ref[0], hi[rs]) + _dot(tri_ref[0], lo[rs])
        bwd = _dot(tri_ref[1], hi[rs]) + _dot(tri_ref[1], lo[rs])
        cum = jnp.where(sel == 1.0, fwd, jnp.where(sel == 2.0, bwd, g[rs]))
        g_ref[0, rs, :] = cum
        gt_ref[0, c] = cum.T[0:GATE_ROWS, :]


def _features(xmz, conv_w, conv_b, wq, wkt, wv, mg, bg, tri, sel, tm):
    bsz, seq, _ = xmz.shape
    nh, dh, _ = wq.shape
    e = nh * dh
    hb = tm // HALO_ROWS
    nhalo = seq // HALO_ROWS
    tok = pl.BlockSpec((1, tm, e), lambda b, i: (b, i, 0))
    t = tri.shape[1]
    nbytes = 2 * tm * e * 2 * 5 + 3 * e * dh * 2 + 2 * e * V7X_LANES * 2 + 10 * tm * dh * 4 + 2 * t * t * 2
    return pl.pallas_call(
        functools.partial(_feat_kernel, kscale=float(dh) ** -0.5),
        grid=(bsz, seq // tm),
        in_specs=[tok,
                  pl.BlockSpec((1, HALO_ROWS, e), lambda b, i: (b, jnp.maximum(i * hb - 1, 0), 0)),
                  pl.BlockSpec((1, HALO_ROWS, e), lambda b, i: (b, jnp.minimum((i + 1) * hb, nhalo - 1), 0)),
                  pl.BlockSpec((MLSTM_CONV_K, e), lambda b, i: (0, 0)),
                  pl.BlockSpec((1, e), lambda b, i: (0, 0)),
                  _resident((nh, dh, dh)), _resident((nh, dh, dh)), _resident((nh, dh, dh)),
                  _resident((2, e, V7X_LANES)),
                  pl.BlockSpec((1, V7X_LANES), lambda b, i: (0, 0)),
                  _resident((2, t, t)),
                  pl.BlockSpec((1, V7X_LANES), lambda b, i: (0, 0))],
        out_specs=[pl.BlockSpec((1, nh, tm, dh), lambda b, i: (b, 0, i, 0)),
                   pl.BlockSpec((1, nh, tm // t, dh, t), lambda b, i: (b, 0, i, 0, 0)),
                   pl.BlockSpec((1, nh, tm, dh), lambda b, i: (b, 0, i, 0)),
                   tok,
                   pl.BlockSpec((1, tm, V7X_LANES), lambda b, i: (b, i, 0)),
                   pl.BlockSpec((1, tm // t, GATE_ROWS, t), lambda b, i: (b, i, 0, 0))],
        out_shape=[jax.ShapeDtypeStruct((bsz, nh, seq, dh), BF16),
                   jax.ShapeDtypeStruct((bsz, nh, seq // t, dh, t), BF16),
                   jax.ShapeDtypeStruct((bsz, nh, seq, dh), BF16),
                   jax.ShapeDtypeStruct((bsz, seq, e), BF16),
                   jax.ShapeDtypeStruct((bsz, seq, V7X_LANES), F32),
                   jax.ShapeDtypeStruct((bsz, seq // t, GATE_ROWS, t), F32)],
        compiler_params=_params(nbytes, 2),
        name="mlstm_features",
    )(xmz, xmz, xmz, conv_w, conv_b.reshape(1, e), wq, wkt, wv, mg, bg, tri, sel)


def _scan_kernel(qf_ref, ktf_ref, vf_ref, gf_ref, gtf_ref, qb_ref, ktb_ref, vb_ref, gb_ref, gtb_ref,
                 ktx_ref, vx_ref, gtx_ref, mask_ref, o_ref, c_ref, n_ref, m_ref, *, nc, nh):
    hg = qf_ref.shape[1]
    h0 = pl.program_id(1) * hg
    c = pl.program_id(2)
    t, dh = qf_ref.shape[2], qf_ref.shape[3]
    lane_g = lax.broadcasted_iota(jnp.int32, (1, V7X_LANES), 1)
    sub = lax.broadcasted_iota(jnp.int32, (V7X_BF16_SUBLANES, t), 0)

    def gate_rows(gt_ref, dd, h):
        ig_r = gt_ref[0, 0, pl.ds(dd * nh + h, 1), :]
        b_r = gt_ref[0, 0, pl.ds(2 * nh + dd * nh + h, 1), :]
        last = (t - 1) * (1 - dd)
        return ig_r, b_r, b_r[:, last:last + 1]

    def update_state(st, kt, vb, ig_r, b_r, b_last):
        m = m_ref[st]
        w_r = b_last - b_r + ig_r
        m_new = jnp.maximum(b_last + m, jnp.max(w_r, axis=1, keepdims=True))
        decay = jnp.exp(b_last + m - m_new)
        wexp = jnp.exp(w_r - m_new)
        kw = kt * wexp.astype(BF16)
        c_ref[st] = decay * c_ref[st] + _dot(kw, vb)
        w_hi = wexp.astype(BF16).astype(F32)
        w2 = jnp.where(sub == 0, w_hi, jnp.where(sub == 1, wexp - w_hi, 0.0)).astype(BF16)
        n_ref[st] = decay * n_ref[st] + jnp.sum(_dot_nt(w2, kt), axis=0, keepdims=True)
        m_ref[st] = m_new

    @pl.when(c == 0)
    def _():
        c_ref[...] = jnp.zeros(c_ref.shape, F32)
        n_ref[...] = jnp.zeros(n_ref.shape, F32)
        m_ref[...] = jnp.zeros(m_ref.shape, F32)
        o_ref[...] = jnp.zeros(o_ref.shape, o_ref.dtype)
        for hh in range(hg):
            for dd in range(2):
                update_state(2 * hh + dd, ktx_ref[0, hh, 0], vx_ref[0, hh], *gate_rows(gtx_ref, dd, h0 + hh))

    streams = ((qf_ref, ktf_ref, vf_ref, gf_ref, gtf_ref), (qb_ref, ktb_ref, vb_ref, gb_ref, gtb_ref))
    rows = (pl.ds(pl.multiple_of(c * t, t), t), pl.ds(pl.multiple_of((nc - 1 - c) * t, t), t))
    outs, pending = [], []
    for hh in range(hg):
        for dd, (q_ref, kt_ref, v_ref, g_ref, gt_ref) in enumerate(streams):
            st = 2 * hh + dd
            qb, kt, vb = q_ref[0, hh], kt_ref[0, hh, 0], v_ref[0, hh]
            ig_r, b_r, b_last = gate_rows(gt_ref, dd, h0 + hh)
            b_c = jnp.sum(jnp.where(lane_g == 2 * nh + dd * nh + h0 + hh, g_ref[0], 0.0),
                          axis=1, keepdims=True)
            m = m_ref[st]
            dmat = jnp.where(mask_ref[dd] > 0.0, b_c - b_r + ig_r, -jnp.inf)
            inter = b_c + m
            m_t = jnp.maximum(inter, jnp.max(dmat, axis=1, keepdims=True))
            s = _dot(qb, kt) * jnp.exp(dmat - m_t)
            iw = jnp.exp(inter - m_t)
            num = iw * _dot(qb, c_ref[st].astype(BF16)) + _dot(s.astype(BF16), vb)
            den = (iw * jnp.sum(qb.astype(F32) * n_ref[st], axis=1, keepdims=True)
                   + jnp.sum(s, axis=1, keepdims=True))
            outs.append((hh, dd, num * (1.0 / jnp.maximum(jnp.abs(den), jnp.exp(-m_t)))))
            pending.append((st, kt, vb, ig_r, b_r, b_last))

    for hh, dd, hout in outs:
        cs = slice(hh * dh, (hh + 1) * dh)
        o_ref[0, rows[dd], cs] = (o_ref[0, rows[dd], cs].astype(F32) + hout).astype(o_ref.dtype)

    for args in pending:
        update_state(*args)


def _scan(q, kt, v, g, gt, ktx, vx, gtx, mask, hg):
    bsz, nh, seq, dh = q.shape
    t = SCAN_CHUNK
    nc = seq // t
    assert nc % 2 == 0, "both directions advance together: needs an even number of chunks"
    assert vx.shape[2] == t, "context length must equal one scan chunk"
    fwd = lambda c: c
    bwd = lambda c: nc - 1 - c
    zero = lambda c: 0
    tok = lambda f: pl.BlockSpec((1, hg, t, dh), lambda b, h, c: (b, h, f(c), 0))
    ktb = lambda f: pl.BlockSpec((1, hg, 1, dh, t), lambda b, h, c: (b, h, f(c), 0, 0))
    gcol = lambda f: pl.BlockSpec((1, t, V7X_LANES), lambda b, h, c: (b, f(c), 0))
    grow = lambda f: pl.BlockSpec((1, 1, GATE_ROWS, t), lambda b, h, c: (b, f(c), 0, 0))
    nbytes = hg * (2 * seq * dh * 2 + 16 * t * dh * 2 + 4 * dh * dh * 4 + 24 * t * t * 4
                   + 8 * t * dh * 4) + 4 * t * V7X_LANES * 4 + 2 * t * t * 4
    return pl.pallas_call(
        functools.partial(_scan_kernel, nc=nc, nh=nh),
        grid=(bsz, nh // hg, nc),
        in_specs=[tok(fwd), ktb(fwd), tok(fwd), gcol(fwd), grow(fwd),
                  tok(bwd), ktb(bwd), tok(bwd), gcol(bwd), grow(bwd),
                  ktb(zero), tok(zero), grow(zero), _resident((2, t, t))],
        out_specs=pl.BlockSpec((1, seq, hg * dh), lambda b, h, c: (b, 0, h)),
        out_shape=jax.ShapeDtypeStruct((bsz, seq, nh * dh), BF16),
        scratch_shapes=[pltpu.VMEM((2 * hg, dh, dh), F32), pltpu.VMEM((2 * hg, 1, dh), F32),
                        pltpu.VMEM((2 * hg, 1, 1), F32)],
        compiler_params=_params(nbytes, 3),
        name="mlstm_scan",
    )(q, kt, v, g, gt, q, kt, v, g, gt, ktx, vx, gtx, mask)


def _ml_ffn_kernel(hs_ref, xc_ref, sz_ref, h_ref, ng_ref, sk_ref, wml_ref, md_ref, wi_ref, wo_ref,
                   ln_ref, o_ref, *, alpha, fc, sub, nh):
    e = hs_ref.shape[2]
    dh = e // nh

    def prologue(rs):
        acc = jnp.zeros((sub, h_ref.shape[2]), F32)
        for h in range(nh):
            cs = slice(h * dh, (h + 1) * dh)
            hh = hs_ref[0, rs, cs].astype(F32)
            mu = jnp.mean(hh, axis=-1, keepdims=True)
            hc = hh - mu
            var = jnp.mean(hc * hc, axis=-1, keepdims=True)
            hn = (hc * (lax.rsqrt(var + LN_EPS) * ng_ref[:, cs])).astype(BF16)
            y = (hn + sk_ref[:, cs].astype(BF16) * xc_ref[0, rs, cs]) * sz_ref[0, rs, cs]
            acc = acc + _dot(y, wml_ref[cs, :])
        return _layer_norm(alpha * h_ref[0, rs, :] + md_ref[0, 2:3, :] * acc, ln_ref[0:1, :], ln_ref[1:2, :])

    _ffn_subtiles(prologue, o_ref, sub, md_ref, wi_ref, wo_ref, ln_ref, alpha, fc)


def _ml_ffn(hs, xc, xmz, h, norm_g, skip, w_ml, md, w_in, w_out, ln, alpha, tm):
    bsz, seq, d = h.shape
    e = hs.shape[2]
    f = w_out.shape[0]
    tok = lambda n, j=0: pl.BlockSpec((1, tm, n), lambda b, i: (b, i, j))
    rowe = pl.BlockSpec((1, e), lambda b, i: (0, 0))
    nbytes = 2 * tm * e * 6 + 4 * tm * d * 4 + e * d * 2 + 3 * d * f * 2 + 8 * tm * d * 4
    return pl.pallas_call(
        functools.partial(_ml_ffn_kernel, alpha=alpha, fc=V7X_MXU_DIM, sub=min(tm, FFN_SUB_ROWS), nh=MLSTM_HEADS),
        grid=(bsz, seq // tm),
        in_specs=[tok(e), tok(e), tok(e, 1), tok(d), rowe, rowe, _resident((e, d)),
                  pl.BlockSpec((1,) + md.shape[1:], lambda b, i: (b, 0, 0)),
                  _resident((d, 2 * f)), _resident((f, d)),
                  pl.BlockSpec(ln.shape, lambda b, i: (0, 0))],
        out_specs=tok(d),
        out_shape=jax.ShapeDtypeStruct((bsz, seq, d), F32),
        compiler_params=_params(nbytes, 2),
        name="mlstm_out_ffn",
    )(hs, xc, xmz, h, norm_g.reshape(1, e), skip.reshape(1, e), w_ml, md, w_in, w_out, ln)


def _tokens(a, shared_md):
    return a.reshape(1, -1, a.shape[-1]) if shared_md else a


def _pool_conv_layer(h, md, ab, ffn_wi, ffn_wo, ln, alpha, grid2d, tm, tm_tok):
    w_in, pool_w, pool_b, pool_scale, conv_w, conv_b, norm_g, norm_b, w_out = ab
    seq = h.shape[1]
    shared = md.shape[0] == 1 and h.shape[0] > 1
    pool_width = pool_w.shape[0] * POOL_GROUP
    p = _modmm(_tokens(h, shared), md, w_in, tm_tok).reshape(h.shape[:2] + (w_in.shape[1],))
    band, inv = _pool_constants(seq, grid2d)
    ng = len(POOL_WINDOWS)
    ya = _pool_mixer(p, band, inv, pool_w, pool_b.reshape(ng, 1, POOL_GROUP),
                     pool_scale.reshape(ng, 1, POOL_GROUP), grid2d)
    yb = _conv_mixer(p, pool_width, conv_w, conv_b, norm_g, norm_b, tm)
    out = _ab_ffn(_tokens(ya, shared), _tokens(yb, shared), _tokens(h, shared), w_out, md, ffn_wi, ffn_wo,
                  ln, alpha, tm_tok)
    return out.reshape(h.shape)


def kernel(x, c, ctx, c_ctx, mod_w, mod_b, ln_g, ln_b, ab_w_in, ab_pool_w, ab_pool_b, ab_pool_scale, ab_conv_w, ab_conv_b, ab_norm_g, ab_norm_b, ab_w_out, ml_w_in, ml_conv_w, ml_conv_b, ml_wq, ml_wk, ml_wv, ml_w_gate, ml_b_gate, ml_norm_g, ml_skip, ml_w_out, ffn_w_in, ffn_w_out):
    bsz, seq, d = x.shape
    depth = mod_w.shape[0]
    nh = MLSTM_HEADS
    alpha = (2.0 * depth) ** 0.25
    tm_x, tm_c = 512, ctx.shape[1]
    tm_wide = 2 * tm_x
    tm_ctok = min(tm_wide, bsz * tm_c)

    n_rows = -(-(bsz + 1) // V7X_SUBLANES) * V7X_SUBLANES
    rows = jnp.concatenate([c, c_ctx[None, :], jnp.zeros((n_rows - bsz - 1, d), F32)], axis=0)
    mods = _mods(rows, mod_w, mod_b)

    h, hc = x, ctx
    for l in range(depth):
        last = l == depth - 1
        even = l % 2 == 0
        j = l // 2
        md_x = mods[l, :bsz].reshape(bsz, 6, d)
        md_c = mods[l, bsz].reshape(1, 6, d)
        ln = jnp.stack([ln_g[l, 0], ln_b[l, 0], ln_g[l, 1], ln_b[l, 1]])
        ffn_wi, ffn_wo = ffn_w_in[l].astype(BF16), ffn_w_out[l].astype(BF16)
        if even:
            ab = (ab_w_in[j].astype(BF16), ab_pool_w[j].astype(BF16), ab_pool_b[j], ab_pool_scale[j],
                  ab_conv_w[j], ab_conv_b[j], ab_norm_g[j], ab_norm_b[j], ab_w_out[j].astype(BF16))
            h = _pool_conv_layer(h, md_x, ab, ffn_wi, ffn_wo, ln, alpha, True, tm_wide, tm_wide)
            if not last:
                hc = _pool_conv_layer(hc, md_c, ab, ffn_wi, ffn_wo, ln, alpha, False, tm_c, tm_ctok)
        else:
            e = ml_w_in.shape[2] // 2
            w_in = ml_w_in[j].astype(BF16)
            wq, wv = ml_wq[j].astype(BF16), ml_wv[j].astype(BF16)
            wkt = jnp.swapaxes(ml_wk[j], 1, 2).astype(BF16)
            wg = jnp.transpose(ml_w_gate[j].reshape(2, 3, e, 2, nh), (1, 2, 3, 0, 4)).reshape(3, e, GATE_ROWS)
            wg = jnp.pad(wg, ((0, 0), (0, 0), (0, V7X_LANES - GATE_ROWS)))
            mg = _gatefold(ml_wq[j], ml_wk[j], ml_wv[j], wg)
            bg = jnp.transpose(ml_b_gate[j].reshape(2, 2, nh), (1, 0, 2)).reshape(1, GATE_ROWS)
            bg = jnp.pad(bg, ((0, 0), (0, V7X_LANES - GATE_ROWS)))
            t = SCAN_CHUNK
            idx = jnp.arange(t)
            lower = (idx[None, :] <= idx[:, None])
            tri = jnp.stack([lower, lower.T]).astype(BF16)
            mask = jnp.stack([lower, lower.T]).astype(F32)
            col = jnp.arange(V7X_LANES)
            sel = jnp.where((col >= 2 * nh) & (col < 3 * nh), 1.0,
                            jnp.where((col >= 3 * nh) & (col < 4 * nh), 2.0, 0.0)).reshape(1, V7X_LANES)
            feat = (ml_conv_w[j], ml_conv_b[j], wq, wkt, wv, mg, bg, tri, sel)
            if not last:
                raise NotImplementedError("context output of an mLSTM layer is only needed for depth > 2")
            xm_c = _modmm(_tokens(hc, True), md_c, w_in[:, :e], tm_ctok).reshape(hc.shape[:2] + (e,))
            _, kt_c, v_c, _, _, gt_c = _features(xm_c, *feat, tm_c)
            xmz = _modmm(h, md_x, w_in, tm_wide, silu_from=e)
            q, kt, v, xc, g, gt = _features(xmz, *feat, tm_x)
            hs = _scan(q, kt, v, g, gt, kt_c, v_c, gt_c, mask, hg=2)
            h = _ml_ffn(hs, xc, xmz, h, ml_norm_g[j], ml_skip[j], ml_w_out[j].astype(BF16), md_x,
                        ffn_wi, ffn_wo, ln, alpha, tm_x)
    return h
```
